```python
import math
import jax, jax.numpy as jnp
from jax import lax
import numpy as np

D_MODEL = 2048
BATCH = 4
SEQ = 4096
DEPTH = 2

CHUNK = 64
N_META = 16
META_OFFSET = CHUNK - N_META
Q_BLOCK = 128
EPS = 1e-6
ROPE_THETA = 10000.0

GLA_HEADS = 4
GLA_DK = 128
GLA_DV = 256
GLA_GATE_RANK = 16
GLA_GATE_NORM = 16.0

RET_HEADS = 4
RET_DK = 256
RET_DV = 256

DIFF_HEADS = 8
DIFF_DK = 128
DIFF_DV = 256

D_FF = 5632
CONV_W = 3

GLA_SPLITS = (GLA_HEADS * GLA_DK, GLA_HEADS * GLA_DK, GLA_HEADS * GLA_DV, GLA_HEADS * GLA_DV, GLA_GATE_RANK)
RET_SPLITS = (RET_HEADS * RET_DK, RET_HEADS * RET_DK, RET_HEADS * RET_DV, RET_HEADS * RET_DV)
EVEN_IN = sum(GLA_SPLITS) + sum(RET_SPLITS)
EVEN_MIX = GLA_HEADS * GLA_DV + RET_HEADS * RET_DV
DIFF_SPLITS = (2 * DIFF_HEADS * DIFF_DK, 2 * DIFF_HEADS * DIFF_DK, DIFF_HEADS * DIFF_DV)
ODD_IN = sum(DIFF_SPLITS)
ODD_MIX = DIFF_HEADS * DIFF_DV

kernel_name = 'hybrid_gla_retnet_diffattn_convffn_meta'


def rms_norm(x, g):
    xf = x.astype(jnp.float32)
    y = xf * lax.rsqrt(jnp.mean(xf * xf, axis=-1, keepdims=True) + EPS)
    return (y * g.astype(jnp.float32)).astype(x.dtype)


def split_cols(t, sizes):
    offs = np.cumsum(sizes)[:-1]
    return jnp.split(t, [int(o) for o in offs], axis=-1)


def to_heads(t, n_heads):
    b, l, _ = t.shape
    return t.reshape(b, l, n_heads, -1).transpose(0, 2, 1, 3)


def from_heads(t):
    b, n, l, d = t.shape
    return t.transpose(0, 2, 1, 3).reshape(b, l, n * d)


def rotate(x, pos, inv_freq):
    ang = pos.astype(jnp.float32)[:, None] * inv_freq[None, :]
    cos, sin = jnp.cos(ang), jnp.sin(ang)
    x1, x2 = jnp.split(x.astype(jnp.float32), 2, axis=-1)
    return jnp.concatenate([x1 * cos - x2 * sin, x2 * cos + x1 * sin], axis=-1).astype(x.dtype)


def to_chunks(t):
    b, h, l, d = t.shape
    return t.reshape(b, h, l // CHUNK, CHUNK, d).transpose(2, 0, 1, 3, 4)


def from_chunks(t):
    nc, b, h, c, d = t.shape
    return t.transpose(1, 2, 0, 3, 4).reshape(b, h, nc * c, d)


def gla_chunk_scan(q, k, v, log_a):
    b, h, _, dk = q.shape
    dv = v.shape[-1]

    def step(state, inp):
        qc, kc, vc, la = inp
        cum = jnp.cumsum(la, axis=-2)
        pair = jnp.exp(-jnp.abs(cum[:, :, :, None, :] - cum[:, :, None, :, :]))
        scores = jnp.einsum('bhid,bhjd,bhijd->bhij', qc, kc, pair)
        o = jnp.einsum('bhij,bhjv->bhiv', scores, vc) + jnp.einsum('bhid,bhdv->bhiv', qc * jnp.exp(cum), state)
        last = cum[:, :, -1:, :]
        state = state * jnp.exp(last)[:, :, 0, :, None] + jnp.einsum('bhjd,bhjv->bhdv', kc * jnp.exp(last - cum), vc)
        return state, o

    init = jnp.zeros((b, h, dk, dv), jnp.float32)
    _, o = lax.scan(step, init, (to_chunks(q), to_chunks(k), to_chunks(v), to_chunks(log_a)))
    return from_chunks(o)


def retention_chunk_scan(q, k, v, log_gamma):
    b, h, _, dk = q.shape
    dv = v.shape[-1]
    n = jnp.arange(CHUNK, dtype=jnp.float32)
    intra = jnp.exp(jnp.abs(n[:, None] - n[None, :])[None] * log_gamma[:, None, None])
    xi = jnp.exp((n[None, :] + 1.0) * log_gamma[:, None])[:, :, None]
    zeta = jnp.exp((CHUNK - 1.0 - n[None, :]) * log_gamma[:, None])[:, :, None]
    g_chunk = jnp.exp(CHUNK * log_gamma)[:, None, None]

    def step(state, inp):
        qc, kc, vc = inp
        o = (jnp.einsum('bhij,bhjv->bhiv', jnp.einsum('bhid,bhjd->bhij', qc, kc) * intra, vc)
             + jnp.einsum('bhid,bhdv->bhiv', qc, state) * xi)
        state = state * g_chunk + jnp.einsum('bhjd,bhjv->bhdv', kc * zeta, vc)
        return state, o

    init = jnp.zeros((b, h, dk, dv), jnp.float32)
    _, o = lax.scan(step, init, (to_chunks(q), to_chunks(k), to_chunks(v)))
    return from_chunks(o)


def diff_attention(q1, q2, k1, k2, v, lam, chunk_id, key_valid):
    b, h, l, d = q1.shape
    nq = l // Q_BLOCK
    scale = d ** -0.5

    def blocks(t):
        return t.reshape(b, h, nq, Q_BLOCK, t.shape[-1]).transpose(2, 0, 1, 3, 4)

    def one_block(args):
        q1b, q2b, qcid = args
        mask = (chunk_id[None, :] <= qcid[:, None]) & key_valid[None, :]

        def probs(qb, kk):
            s = jnp.einsum('bhqd,bhkd->bhqk', qb, kk).astype(jnp.float32) * scale
            return jax.nn.softmax(jnp.where(mask, s, -1e30), axis=-1)

        a = probs(q1b, k1) - lam * probs(q2b, k2)
        return jnp.einsum('bhqk,bhkv->bhqv', a.astype(v.dtype), v)

    o = lax.map(one_block, (blocks(q1), blocks(q2), chunk_id.reshape(nq, Q_BLOCK)))
    return o.transpose(1, 2, 0, 3, 4).reshape(b, h, l, -1)


def even_mixer(hn, valid, pos, w_in, w_gate2, b_gate, gla_norm, ret_norm, w_out):
    dtype = hn.dtype
    f32 = jnp.float32
    qa, ka, va, ga, lr, qb, kb, vb, gb = split_cols(hn @ w_in, GLA_SPLITS + RET_SPLITS)
    vm = valid.astype(f32)[None, None, :, None]
    log_a = jax.nn.log_sigmoid((lr @ w_gate2 + b_gate).astype(f32)) / GLA_GATE_NORM
    o_a = gla_chunk_scan(to_heads(qa, GLA_HEADS).astype(f32) * (GLA_DK ** -0.5),
                         to_heads(ka, GLA_HEADS).astype(f32) * vm,
                         to_heads(va, GLA_HEADS).astype(f32),
                         to_heads(log_a, GLA_HEADS) * vm)
    o_a = from_heads(rms_norm(o_a.astype(dtype), gla_norm[:, None, :])) * jax.nn.silu(ga)
    inv_freq_ret = 1.0 / (ROPE_THETA ** jnp.linspace(0.0, 1.0, RET_DK // 2, dtype=f32))
    log_gamma = jnp.log(1.0 - 2.0 ** (-5.0 - jnp.arange(RET_HEADS, dtype=f32)))
    qr = rotate(to_heads(qb, RET_HEADS), pos, inv_freq_ret).astype(f32)
    kr = rotate(to_heads(kb, RET_HEADS), pos, inv_freq_ret).astype(f32) * (RET_DK ** -0.5) * vm
    o_b = retention_chunk_scan(qr, kr, to_heads(vb, RET_HEADS).astype(f32), log_gamma)
    o_b = from_heads(rms_norm(o_b.astype(dtype), ret_norm[:, None, :])) * jax.nn.silu(gb)
    return jnp.concatenate([o_a, o_b], axis=-1) @ w_out


def odd_mixer(hn, layer, valid, pos, chunk_id, w_in, q_norm, k_norm, lam_q1, lam_k1, lam_q2, lam_k2, out_norm, w_out):
    b, l, _ = hn.shape
    f32 = jnp.float32
    q, k, v = split_cols(hn @ w_in, DIFF_SPLITS)
    inv_freq = 1.0 / (ROPE_THETA ** (jnp.arange(0, DIFF_DK, 2, dtype=f32) / DIFF_DK))

    def qk(t, g):
        t = rms_norm(t.reshape(b, l, DIFF_HEADS, 2, DIFF_DK), g)
        t = rotate(t.transpose(0, 3, 2, 1, 4), pos, inv_freq)
        return t[:, 0], t[:, 1]

    q1, q2 = qk(q, q_norm)
    k1, k2 = qk(k, k_norm)
    lam_init = 0.8 - 0.6 * math.exp(-0.3 * layer)
    lam = (jnp.exp(jnp.sum(lam_q1.astype(f32) * lam_k1.astype(f32)))
           - jnp.exp(jnp.sum(lam_q2.astype(f32) * lam_k2.astype(f32))) + lam_init)
    o = diff_attention(q1, q2, k1, k2, to_heads(v, DIFF_HEADS), lam, chunk_id, valid)
    o = rms_norm(o, out_norm) * (1.0 - lam_init)
    return from_heads(o) @ w_out


def conv_ffn(hn, w_up, conv_w, conv_b, w_down):
    u = hn @ w_up
    l = u.shape[1]
    up = jnp.pad(u, ((0, 0), (CONV_W - 1, 0), (0, 0)))
    c = conv_b
    for t in range(CONV_W):
        c = c + conv_w[t] * up[:, t:t + l]
    val, gate = jnp.split(c, 2, axis=-1)
    return (val * jax.nn.silu(gate)) @ w_down


def setup_inputs(seed: int = 0) -> dict:
    key = jax.random.key(seed)
    ks = iter(jax.random.split(key, 32))
    n_even = (DEPTH + 1) // 2
    n_odd = DEPTH // 2

    def nrm(shape, scale):
        return scale * jax.random.normal(next(ks), shape, jnp.float32)

    def gain(shape):
        return 1.0 + 0.1 * jax.random.normal(next(ks), shape, jnp.float32)

    return {
        'x': nrm((BATCH, SEQ, D_MODEL), 1.0),
        'meta': nrm((N_META, D_MODEL), 1.0),
        'mix_norm_e': gain((n_even, D_MODEL)),
        'w_in_e': nrm((n_even, D_MODEL, EVEN_IN), D_MODEL ** -0.5),
        'gla_w_gate_e': nrm((n_even, GLA_GATE_RANK, GLA_HEADS * GLA_DK), GLA_GATE_RANK ** -0.5),
        'gla_b_gate_e': nrm((n_even, GLA_HEADS * GLA_DK), 0.1),
        'gla_norm_e': gain((n_even, GLA_HEADS, GLA_DV)),
        'ret_norm_e': gain((n_even, RET_HEADS, RET_DV)),
        'w_out_e': nrm((n_even, EVEN_MIX, D_MODEL), EVEN_MIX ** -0.5),
        'mix_norm_o': gain((n_odd, D_MODEL)),
        'w_in_o': nrm((n_odd, D_MODEL, ODD_IN), D_MODEL ** -0.5),
        'q_norm_o': gain((n_odd, DIFF_DK)),
        'k_norm_o': gain((n_odd, DIFF_DK)),
        'lam_q1_o': nrm((n_odd, DIFF_DK), 0.1),
        'lam_k1_o': nrm((n_odd, DIFF_DK), 0.1),
        'lam_q2_o': nrm((n_odd, DIFF_DK), 0.1),
        'lam_k2_o': nrm((n_odd, DIFF_DK), 0.1),
        'diff_norm_o': gain((n_odd, DIFF_DV)),
        'w_out_o': nrm((n_odd, ODD_MIX, D_MODEL), ODD_MIX ** -0.5),
        'ffn_norm': gain((DEPTH, D_MODEL)),
        'w_up': nrm((DEPTH, D_MODEL, 2 * D_FF), D_MODEL ** -0.5),
        'conv_w': nrm((DEPTH, CONV_W, 2 * D_FF), CONV_W ** -0.5),
        'conv_b': nrm((DEPTH, 2 * D_FF), 0.02),
        'w_down': nrm((DEPTH, D_FF, D_MODEL), D_FF ** -0.5),
    }


def reference(x, meta, mix_norm_e, w_in_e, gla_w_gate_e, gla_b_gate_e, gla_norm_e, ret_norm_e, w_out_e,
              mix_norm_o, w_in_o, q_norm_o, k_norm_o, lam_q1_o, lam_k1_o, lam_q2_o, lam_k2_o, diff_norm_o, w_out_o,
              ffn_norm, w_up, conv_w, conv_b, w_down):
    b, s, d = x.shape
    lp = -(-(CHUNK + s) // Q_BLOCK) * Q_BLOCK
    h = jnp.concatenate([jnp.zeros((b, META_OFFSET, d), x.dtype),
                         jnp.broadcast_to(meta.astype(x.dtype)[None], (b, N_META, d)),
                         x,
                         jnp.zeros((b, lp - CHUNK - s, d), x.dtype)], axis=1)
    idx = jnp.arange(lp)
    valid = (idx >= META_OFFSET) & (idx < CHUNK + s)
    pos = idx - META_OFFSET
    chunk_id = idx // CHUNK
    keep = valid.astype(x.dtype)[None, :, None]
    for layer in range(DEPTH):
        i = layer // 2
        if layer % 2 == 0:
            mix = even_mixer(rms_norm(h, mix_norm_e[i]), valid, pos, w_in_e[i], gla_w_gate_e[i], gla_b_gate_e[i],
                             gla_norm_e[i], ret_norm_e[i], w_out_e[i])
        else:
            mix = odd_mixer(rms_norm(h, mix_norm_o[i]), layer, valid, pos, chunk_id, w_in_o[i], q_norm_o[i], k_norm_o[i],
                            lam_q1_o[i], lam_k1_o[i], lam_q2_o[i], lam_k2_o[i], diff_norm_o[i], w_out_o[i])
        h = h + mix * keep
        h = h + conv_ffn(rms_norm(h, ffn_norm[layer]), w_up[layer], conv_w[layer], conv_b[layer], w_down[layer]) * keep
    return h[:, CHUNK:CHUNK + s]
```

```python
import functools
import math

import jax
import jax.numpy as jnp
from jax import lax
from jax.experimental import pallas as pl
from jax.experimental.pallas import tpu as pltpu

F32 = jnp.float32
BF16 = jnp.bfloat16

D_MODEL = 2048
CHUNK = 64
N_META = 16
META_OFFSET = CHUNK - N_META
EPS = 1e-6
ROPE_THETA = 10000.0

GLA_HEADS = 4
GLA_DK = 128
GLA_DV = 256
GLA_GATE_RANK = 16
GLA_GATE_NORM = 16.0

RET_HEADS = 4
RET_DK = 256
RET_DV = 256

DIFF_HEADS = 8
DIFF_DK = 128
DIFF_DV = 256

D_FF = 5632
CONV_W = 3

LANES = 128
ATTN_TILE = 256
ROW_TILE = 512
HALO = 16
FF_TILE = 512
VMEM_LIMIT = 56 * 2**20


def _params(semantics):
    return pltpu.CompilerParams(dimension_semantics=semantics, vmem_limit_bytes=VMEM_LIMIT)


def _rms(x, g):
    return x * lax.rsqrt(jnp.mean(x * x, axis=-1, keepdims=True) + EPS) * g


def _sigmoid(x):
    return 1.0 / (1.0 + jnp.exp(-x))


def _norm_matmul_kernel(x_ref, g_ref, w_ref, o_ref, xn_ref):
    @pl.when(pl.program_id(1) == 0)
    def _():
        xn_ref[...] = _rms(x_ref[...], g_ref[...]).astype(BF16)

    o_ref[...] = jnp.dot(xn_ref[...], w_ref[...], preferred_element_type=F32).astype(o_ref.dtype)


def _norm_matmul(x, g, w, tn):
    m, d = x.shape
    n = w.shape[1]
    return pl.pallas_call(
        _norm_matmul_kernel,
        grid=(m // ROW_TILE, n // tn),
        in_specs=[
            pl.BlockSpec((ROW_TILE, d), lambda i, j: (i, 0)),
            pl.BlockSpec((1, d), lambda i, j: (0, 0)),
            pl.BlockSpec((d, tn), lambda i, j: (0, j)),
        ],
        out_specs=pl.BlockSpec((ROW_TILE, tn), lambda i, j: (i, j)),
        out_shape=jax.ShapeDtypeStruct((m, n), BF16),
        scratch_shapes=[pltpu.VMEM((ROW_TILE, d), BF16)],
        compiler_params=_params(("parallel", "arbitrary")),
        name="norm_matmul",
    )(x, g.reshape(1, d), w)


def _matmul_res_kernel(a_ref, w_ref, h_ref, keep_ref, o_ref):
    mix = jnp.dot(a_ref[...], w_ref[...], preferred_element_type=F32)
    o_ref[...] = h_ref[...] + mix * keep_ref[...]


def _matmul_res(a, w, h, keep):
    m, k = a.shape
    n = w.shape[1]
    return pl.pallas_call(
        _matmul_res_kernel,
        grid=(m // ROW_TILE,),
        in_specs=[
            pl.BlockSpec((ROW_TILE, k), lambda i: (i, 0)),
            pl.BlockSpec((k, n), lambda i: (0, 0)),
            pl.BlockSpec((ROW_TILE, n), lambda i: (i, 0)),
            pl.BlockSpec((ROW_TILE, 1), lambda i: (i, 0)),
        ],
        out_specs=pl.BlockSpec((ROW_TILE, n), lambda i: (i, 0)),
        out_shape=jax.ShapeDtypeStruct((m, n), F32),
        compiler_params=_params(("parallel",)),
        name="matmul_res",
    )(a, w, h, keep)


def _gla_kernel(seq_valid_end, q_ref, k_ref, v_ref, g_ref, lr_ref, w2_ref, b_ref, nrm_ref, sel_ref,
                o_ref, st_ref, cum_ref, qs_ref, ks_ref):
    c = pl.program_id(1)

    @pl.when(c == 0)
    def _():
        st_ref[...] = jnp.zeros_like(st_ref)

    row = c * CHUNK + lax.broadcasted_iota(jnp.int32, (CHUNK, 1), 0)
    vm = jnp.where(row >= META_OFFSET, jnp.where(row < seq_valid_end, 1.0, 0.0), 0.0)
    ii = lax.broadcasted_iota(jnp.int32, (CHUNK, CHUNK), 0)
    jj = lax.broadcasted_iota(jnp.int32, (CHUNK, CHUNK), 1)
    tril = jnp.where(jj <= ii, 1.0, 0.0).astype(F32)

    gate = jnp.dot(lr_ref[...], w2_ref[...], preferred_element_type=F32) + b_ref[...]
    log_a = (jnp.minimum(gate, 0.0) - jnp.log(1.0 + jnp.exp(-jnp.abs(gate)))) * (1.0 / GLA_GATE_NORM) * vm

    for h in range(GLA_HEADS):
        ksl = slice(h * GLA_DK, (h + 1) * GLA_DK)
        cum_ref[h] = jnp.dot(tril, log_a[:, ksl], preferred_element_type=F32,
                             precision=lax.Precision.HIGHEST)
        qs_ref[h] = q_ref[:, ksl].astype(F32) * (GLA_DK ** -0.5)
        ks_ref[h] = k_ref[:, ksl].astype(F32) * vm

    def pair_body(j, scores):
        sel = sel_ref[j]
        new = []
        for h in range(GLA_HEADS):
            cum = cum_ref[h]
            w = jnp.exp(-jnp.abs(cum - cum_ref[h, pl.ds(j, 1), :])) * (qs_ref[h] * ks_ref[h, pl.ds(j, 1), :])
            new.append(scores[h] + jnp.dot(w.astype(BF16), sel, preferred_element_type=F32))
        return tuple(new)

    scores = lax.fori_loop(0, CHUNK, pair_body,
                           tuple(jnp.zeros((CHUNK, CHUNK), F32) for _ in range(GLA_HEADS)))

    for h in range(GLA_HEADS):
        vsl = slice(h * GLA_DV, (h + 1) * GLA_DV)
        cum = cum_ref[h]
        v = v_ref[:, vsl]
        st = st_ref[h]
        q_in = (qs_ref[h] * jnp.exp(cum)).astype(BF16)
        o = jnp.dot(scores[h].astype(BF16), v, preferred_element_type=F32)
        o = o + lax.dot_general(q_in, st.astype(BF16), (((1,), (1,)), ((), ())), preferred_element_type=F32)
        last = cum[CHUNK - 1:CHUNK, :]
        k_dec = (ks_ref[h] * jnp.exp(last - cum)).astype(BF16)
        st_ref[h] = st * jnp.exp(last) + lax.dot_general(v, k_dec, (((0,), (0,)), ((), ())),
                                                         preferred_element_type=F32)
        gv = g_ref[:, vsl].astype(F32)
        o_ref[:, vsl] = (_rms(o, nrm_ref[h]) * (gv * _sigmoid(gv))).astype(o_ref.dtype)


def _gla(proj, lr, w2, b_gate, nrm, sel, lpad, seq_valid_end):
    nb = proj.shape[0]
    hk = GLA_HEADS * GLA_DK
    hv = GLA_HEADS * GLA_DV
    return pl.pallas_call(
        functools.partial(_gla_kernel, seq_valid_end),
        grid=(nb, lpad // CHUNK),
        in_specs=[
            pl.BlockSpec((None, CHUNK, hk), lambda b, c: (b, c, 0)),
            pl.BlockSpec((None, CHUNK, hk), lambda b, c: (b, c, 1)),
            pl.BlockSpec((None, CHUNK, hv), lambda b, c: (b, c, 1)),
            pl.BlockSpec((None, CHUNK, hv), lambda b, c: (b, c, 2)),
            pl.BlockSpec((None, CHUNK, LANES), lambda b, c: (b, c, 0)),
            pl.BlockSpec((LANES, hk), lambda b, c: (0, 0)),
            pl.BlockSpec((1, hk), lambda b, c: (0, 0)),
            pl.BlockSpec((GLA_HEADS, 1, GLA_DV), lambda b, c: (0, 0, 0)),
            pl.BlockSpec((CHUNK, GLA_DK, CHUNK), lambda b, c: (0, 0, 0)),
        ],
        out_specs=pl.BlockSpec((None, CHUNK, hv), lambda b, c: (b, c, 0)),
        out_shape=jax.ShapeDtypeStruct((nb, lpad, hv), BF16),
        scratch_shapes=[
            pltpu.VMEM((GLA_HEADS, GLA_DV, GLA_DK), F32),
            pltpu.VMEM((GLA_HEADS, CHUNK, GLA_DK), F32),
            pltpu.VMEM((GLA_HEADS, CHUNK, GLA_DK), F32),
            pltpu.VMEM((GLA_HEADS, CHUNK, GLA_DK), F32),
        ],
        compiler_params=_params(("parallel", "arbitrary")),
        name="gla_scan",
    )(proj, proj, proj, proj, lr, w2, b_gate, nrm, sel)


def _ret_kernel(seq_valid_end, q_ref, k_ref, v_ref, g_ref, cos_ref, sin_ref, lg_ref, nrm_ref,
                o_ref, st_ref):
    c = pl.program_id(1)

    @pl.when(c == 0)
    def _():
        st_ref[...] = jnp.zeros_like(st_ref)

    row = c * CHUNK + lax.broadcasted_iota(jnp.int32, (CHUNK, 1), 0)
    vm = jnp.where(row >= META_OFFSET, jnp.where(row < seq_valid_end, 1.0, 0.0), 0.0)
    n_i = lax.broadcasted_iota(jnp.int32, (CHUNK, CHUNK), 0).astype(F32)
    n_j = lax.broadcasted_iota(jnp.int32, (CHUNK, CHUNK), 1).astype(F32)
    dist = jnp.abs(n_i - n_j)
    n_col = lax.broadcasted_iota(jnp.int32, (CHUNK, 1), 0).astype(F32)
    cos = cos_ref[...]
    sin = sin_ref[...]
    half = RET_DK // 2

    def rot(x):
        x1 = x[:, :half]
        x2 = x[:, half:]
        return jnp.concatenate([x1 * cos - x2 * sin, x2 * cos + x1 * sin], axis=-1)

    for h in range(RET_HEADS):
        ksl = slice(h * RET_DK, (h + 1) * RET_DK)
        vsl = slice(h * RET_DV, (h + 1) * RET_DV)
        lg = lg_ref[h]
        intra = jnp.exp(dist * lg)
        xi = jnp.exp((n_col + 1.0) * lg)
        zeta = jnp.exp((CHUNK - 1.0 - n_col) * lg)
        g_chunk = jnp.exp(CHUNK * lg)
        qr = rot(q_ref[:, ksl].astype(F32)).astype(BF16)
        kr = rot(k_ref[:, ksl].astype(F32)) * (RET_DK ** -0.5) * vm
        v = v_ref[:, vsl]
        st = st_ref[h]
        s = lax.dot_general(qr, kr.astype(BF16), (((1,), (1,)), ((), ())), preferred_element_type=F32) * intra
        o = jnp.dot(s.astype(BF16), v, preferred_element_type=F32)
        o = o + lax.dot_general(qr, st.astype(BF16), (((1,), (1,)), ((), ())), preferred_element_type=F32) * xi
        st_ref[h] = st * g_chunk + lax.dot_general(v, (kr * zeta).astype(BF16), (((0,), (0,)), ((), ())),
                                                   preferred_element_type=F32)
        gv = g_ref[:, vsl].astype(F32)
        o_ref[:, vsl] = (_rms(o, nrm_ref[h]) * (gv * _sigmoid(gv))).astype(o_ref.dtype)


def _ret(proj, cos, sin, lg, nrm, lpad, seq_valid_end):
    nb = proj.shape[0]
    hk = RET_HEADS * RET_DK
    hv = RET_HEADS * RET_DV
    base = (2 * GLA_HEADS * GLA_DK + 2 * GLA_HEADS * GLA_DV) // hk
    return pl.pallas_call(
        functools.partial(_ret_kernel, seq_valid_end),
        grid=(nb, lpad // CHUNK),
        in_specs=[
            pl.BlockSpec((None, CHUNK, hk), lambda b, c: (b, c, base)),
            pl.BlockSpec((None, CHUNK, hk), lambda b, c: (b, c, base + 1)),
            pl.BlockSpec((None, CHUNK, hv), lambda b, c: (b, c, base + 2)),
            pl.BlockSpec((None, CHUNK, hv), lambda b, c: (b, c, base + 3)),
            pl.BlockSpec((CHUNK, RET_DK // 2), lambda b, c: (c, 0)),
            pl.BlockSpec((CHUNK, RET_DK // 2), lambda b, c: (c, 0)),
            pl.BlockSpec((RET_HEADS, 1, 1), lambda b, c: (0, 0, 0)),
            pl.BlockSpec((RET_HEADS, 1, RET_DV), lambda b, c: (0, 0, 0)),
        ],
        out_specs=pl.BlockSpec((None, CHUNK, hv), lambda b, c: (b, c, 0)),
        out_shape=jax.ShapeDtypeStruct((nb, lpad, hv), BF16),
        scratch_shapes=[pltpu.VMEM((RET_HEADS, RET_DV, RET_DK), F32)],
        compiler_params=_params(("parallel", "arbitrary")),
        name="ret_scan",
    )(proj, proj, proj, proj, cos, sin, lg, nrm)


def _qk_prep_kernel(x_ref, cos_ref, sin_ref, gq_ref, gk_ref, o_ref):
    cos = cos_ref[...]
    sin = sin_ref[...]
    n_q = DIFF_HEADS * 2
    for s in range(2 * n_q):
        sl = slice(s * DIFF_DK, (s + 1) * DIFF_DK)
        x = x_ref[:, sl].astype(F32)
        if s < n_q:
            y = _rms(x, gq_ref[...])
        else:
            y = _rms(x, gk_ref[...])
        y = y * cos + pltpu.roll(y, DIFF_DK // 2, axis=1) * sin
        if s < n_q:
            y = y * (DIFF_DK ** -0.5)
        o_ref[:, sl] = y.astype(o_ref.dtype)


def _qk_prep(proj, cos, sin, gq, gk, lpad):
    nb = proj.shape[0]
    width = 4 * DIFF_HEADS * DIFF_DK
    return pl.pallas_call(
        _qk_prep_kernel,
        grid=(nb, lpad // ATTN_TILE),
        in_specs=[
            pl.BlockSpec((None, ATTN_TILE, width), lambda b, t: (b, t, 0)),
            pl.BlockSpec((ATTN_TILE, DIFF_DK), lambda b, t: (t, 0)),
            pl.BlockSpec((ATTN_TILE, DIFF_DK), lambda b, t: (t, 0)),
            pl.BlockSpec((1, DIFF_DK), lambda b, t: (0, 0)),
            pl.BlockSpec((1, DIFF_DK), lambda b, t: (0, 0)),
        ],
        out_specs=pl.BlockSpec((None, ATTN_TILE, width), lambda b, t: (b, t, 0)),
        out_shape=jax.ShapeDtypeStruct((nb, lpad, width), BF16),
        compiler_params=_params(("parallel", "parallel")),
        name="qk_prep",
    )(proj, cos, sin, gq, gk)


def _diff_attn_kernel(seq_valid_end, lam_init, q1_ref, q2_ref, k1_ref, k2_ref, v_ref, lam_ref, nrm_ref, o_ref):
    qi = pl.program_id(2)
    t = ATTN_TILE
    q1 = q1_ref[...]
    q2 = q2_ref[...]
    q_chunk = lax.shift_right_logical(qi * t + lax.broadcasted_iota(jnp.int32, (t, 1), 0), 6)

    def body(j, carry):
        ks = pl.multiple_of(j * t, t)
        k_idx = j * t + lax.broadcasted_iota(jnp.int32, (1, t), 1)
        k_chunk = jnp.where(k_idx >= META_OFFSET,
                            jnp.where(k_idx < seq_valid_end, lax.shift_right_logical(k_idx, 6), 2**30), 2**30)
        mask = k_chunk <= q_chunk
        v = v_ref[pl.ds(ks, t), :]
        out = []
        for q, k_ref, (m, l, acc) in ((q1, k1_ref, carry[0]), (q2, k2_ref, carry[1])):
            s = lax.dot_general(q, k_ref[pl.ds(ks, t), :], (((1,), (1,)), ((), ())), preferred_element_type=F32)
            s = jnp.where(mask, s, -1e30)
            m_new = jnp.maximum(m, jnp.max(s, axis=-1, keepdims=True))
            alpha = jnp.exp(m - m_new)
            p = jnp.exp(s - m_new)
            l_new = alpha * l + jnp.sum(p, axis=-1, keepdims=True)
            acc_new = alpha * acc + jnp.dot(p.astype(BF16), v, preferred_element_type=F32)
            out.append((m_new, l_new, acc_new))
        return tuple(out)

    def init():
        return (jnp.full((t, 1), -1e30, F32), jnp.zeros((t, 1), F32), jnp.zeros((t, DIFF_DV), F32))

    (_, l1, a1), (_, l2, a2) = lax.fori_loop(0, qi + 1, body, (init(), init()))

    lam_v = lam_ref[...]
    lam = (jnp.exp(jnp.sum(lam_v[0:1] * lam_v[1:2], axis=-1, keepdims=True))
           - jnp.exp(jnp.sum(lam_v[2:3] * lam_v[3:4], axis=-1, keepdims=True)) + lam_init)
    o = a1 / l1 - lam * (a2 / l2)
    o_ref[...] = (_rms(o, nrm_ref[...]) * (1.0 - lam_init)).astype(o_ref.dtype)


def _diff_attn(qk, proj, lam_vecs, nrm, lpad, seq_valid_end, lam_init):
    nb = qk.shape[0]
    t = ATTN_TILE
    kb = 2 * DIFF_HEADS
    vb = 4 * DIFF_HEADS * DIFF_DK // DIFF_DV
    return pl.pallas_call(
        functools.partial(_diff_attn_kernel, seq_valid_end, lam_init),
        grid=(nb, DIFF_HEADS, lpad // t),
        in_specs=[
            pl.BlockSpec((None, t, DIFF_DK), lambda b, h, i: (b, i, 2 * h)),
            pl.BlockSpec((None, t, DIFF_DK), lambda b, h, i: (b, i, 2 * h + 1)),
            pl.BlockSpec((None, lpad, DIFF_DK), lambda b, h, i: (b, 0, kb + 2 * h)),
            pl.BlockSpec((None, lpad, DIFF_DK), lambda b, h, i: (b, 0, kb + 2 * h + 1)),
            pl.BlockSpec((None, lpad, DIFF_DV), lambda b, h, i: (b, 0, vb + h)),
            pl.BlockSpec((4, DIFF_DK), lambda b, h, i: (0, 0)),
            pl.BlockSpec((1, DIFF_DV), lambda b, h, i: (0, 0)),
        ],
        out_specs=pl.BlockSpec((None, t, DIFF_DV), lambda b, h, i: (b, i, h)),
        out_shape=jax.ShapeDtypeStruct((nb, lpad, DIFF_HEADS * DIFF_DV), BF16),
        compiler_params=_params(("parallel", "parallel", "arbitrary")),
        name="diff_attn",
    )(qk, qk, qk, qk, proj, lam_vecs, nrm)


def _ffn_kernel(x_ref, halo_ref, g_ref, wv_ref, wg_ref, cwv_ref, cwg_ref, cbv_ref, cbg_ref, wd_ref, keep_ref,
                o_ref, hn_ref, acc_ref):
    i = pl.program_id(0)
    j = pl.program_id(1)
    tm = x_ref.shape[0]

    @pl.when(j == 0)
    def _():
        g = g_ref[...]
        hn_ref[HALO:, :] = _rms(x_ref[...], g).astype(BF16)
        first = jnp.where(i > 0, 1.0, 0.0)
        hn_ref[:HALO, :] = (_rms(halo_ref[...], g) * first).astype(BF16)
        acc_ref[...] = jnp.zeros_like(acc_ref)

    hn = hn_ref[...]

    def conv_branch(w_ref, cw_ref, cb_ref):
        u = jnp.dot(hn, w_ref[...], preferred_element_type=F32)
        cw = cw_ref[...]
        c = cb_ref[...] + cw[0:1] * u[HALO - 2:HALO - 2 + tm]
        c = c + cw[1:2] * u[HALO - 1:HALO - 1 + tm]
        return c + cw[2:3] * u[HALO:HALO + tm]

    val = conv_branch(wv_ref, cwv_ref, cbv_ref)
    gate = conv_branch(wg_ref, cwg_ref, cbg_ref)
    act = (val * (gate * _sigmoid(gate))).astype(BF16)
    acc_ref[...] += jnp.dot(act, wd_ref[...], preferred_element_type=F32)

    @pl.when(j == pl.num_programs(1) - 1)
    def _():
        o_ref[...] = x_ref[...] + acc_ref[...] * keep_ref[...]


def _ffn(x, g, w_up, conv_w, conv_b, w_down, keep):
    m, d = x.shape
    f = w_down.shape[0]
    nf = f // FF_TILE
    tm = ROW_TILE
    halo_blocks = tm // HALO
    return pl.pallas_call(
        _ffn_kernel,
        grid=(m // tm, nf),
        in_specs=[
            pl.BlockSpec((tm, d), lambda i, j: (i, 0)),
            pl.BlockSpec((None, HALO, d), lambda i, j: (jnp.maximum(i * halo_blocks - 1, 0), 0, 0)),
            pl.BlockSpec((1, d), lambda i, j: (0, 0)),
            pl.BlockSpec((d, FF_TILE), lambda i, j: (0, j)),
            pl.BlockSpec((d, FF_TILE), lambda i, j: (0, nf + j)),
            pl.BlockSpec((CONV_W, FF_TILE), lambda i, j: (0, j)),
            pl.BlockSpec((CONV_W, FF_TILE), lambda i, j: (0, nf + j)),
            pl.BlockSpec((1, FF_TILE), lambda i, j: (0, j)),
            pl.BlockSpec((1, FF_TILE), lambda i, j: (0, nf + j)),
            pl.BlockSpec((FF_TILE, d), lambda i, j: (j, 0)),
            pl.BlockSpec((tm, 1), lambda i, j: (i, 0)),
        ],
        out_specs=pl.BlockSpec((tm, d), lambda i, j: (i, 0)),
        out_shape=jax.ShapeDtypeStruct((m, d), F32),
        scratch_shapes=[pltpu.VMEM((tm + HALO, d), BF16), pltpu.VMEM((tm, d), F32)],
        compiler_params=_params(("parallel", "arbitrary")),
        name="conv_ffn",
    )(x, x.reshape(m // HALO, HALO, d), g.reshape(1, d), w_up, w_up, conv_w, conv_w,
      conv_b.reshape(1, -1), conv_b.reshape(1, -1), w_down, keep)


def _rope_tables(pos, inv_freq):
    ang = pos.astype(F32)[:, None] * inv_freq[None, :]
    return jnp.cos(ang), jnp.sin(ang)


def kernel(x, meta, mix_norm_e, w_in_e, gla_w_gate_e, gla_b_gate_e, gla_norm_e, ret_norm_e, w_out_e, mix_norm_o, w_in_o, q_norm_o, k_norm_o, lam_q1_o, lam_k1_o, lam_q2_o, lam_k2_o, diff_norm_o, w_out_o, ffn_norm, w_up, conv_w, conv_b, w_down):
    nb, s, d = x.shape
    seq_valid_end = CHUNK + s
    lpad = -(-seq_valid_end // ATTN_TILE) * ATTN_TILE
    m = nb * lpad

    h = jnp.concatenate([jnp.zeros((nb, META_OFFSET, d), x.dtype),
                         jnp.broadcast_to(meta.astype(x.dtype)[None], (nb, N_META, d)),
                         x,
                         jnp.zeros((nb, lpad - seq_valid_end, d), x.dtype)], axis=1).reshape(m, d)
    idx = jnp.arange(lpad)
    valid = (idx >= META_OFFSET) & (idx < seq_valid_end)
    pos = idx - META_OFFSET
    keep = jnp.tile(valid.astype(F32), nb).reshape(m, 1)

    gla_cols = 2 * GLA_HEADS * GLA_DK + 2 * GLA_HEADS * GLA_DV
    w_in = w_in_e[0]
    w_main = jnp.concatenate([w_in[:, :gla_cols], w_in[:, gla_cols + GLA_GATE_RANK:]], axis=1).astype(BF16)
    w_lr = jnp.pad(w_in[:, gla_cols:gla_cols + GLA_GATE_RANK], ((0, 0), (0, LANES - GLA_GATE_RANK))).astype(BF16)
    w_gate2 = jnp.pad(gla_w_gate_e[0], ((0, LANES - GLA_GATE_RANK), (0, 0))).astype(BF16)
    proj = _norm_matmul(h, mix_norm_e[0], w_main, 1024).reshape(nb, lpad, -1)
    lr = _norm_matmul(h, mix_norm_e[0], w_lr, LANES).reshape(nb, lpad, LANES)
    sel = (jnp.arange(CHUNK)[:, None, None] == jnp.arange(CHUNK)[None, None, :]).astype(BF16)
    sel = jnp.broadcast_to(sel, (CHUNK, GLA_DK, CHUNK))
    o_a = _gla(proj, lr, w_gate2, gla_b_gate_e[0].reshape(1, -1), gla_norm_e[0].reshape(GLA_HEADS, 1, GLA_DV),
               sel, lpad, seq_valid_end)
    inv_freq_ret = 1.0 / (ROPE_THETA ** jnp.linspace(0.0, 1.0, RET_DK // 2, dtype=F32))
    cos_r, sin_r = _rope_tables(pos, inv_freq_ret)
    log_gamma = jnp.log(1.0 - 2.0 ** (-5.0 - jnp.arange(RET_HEADS, dtype=F32))).reshape(RET_HEADS, 1, 1)
    o_b = _ret(proj, cos_r, sin_r, log_gamma, ret_norm_e[0].reshape(RET_HEADS, 1, RET_DV), lpad, seq_valid_end)
    mix_in = jnp.concatenate([o_a, o_b], axis=-1).reshape(m, -1)
    h = _matmul_res(mix_in, w_out_e[0].astype(BF16), h, keep)
    h = _ffn(h, ffn_norm[0], w_up[0].astype(BF16), conv_w[0], conv_b[0], w_down[0].astype(BF16), keep)

    layer = 1
    lam_init = 0.8 - 0.6 * math.exp(-0.3 * layer)
    proj = _norm_matmul(h, mix_norm_o[0], w_in_o[0].astype(BF16), 1024).reshape(nb, lpad, -1)
    inv_freq = 1.0 / (ROPE_THETA ** (jnp.arange(0, DIFF_DK, 2, dtype=F32) / DIFF_DK))
    cos_d, sin_d = _rope_tables(pos, inv_freq)
    cos_full = jnp.concatenate([cos_d, cos_d], axis=-1)
    sin_signed = jnp.concatenate([-sin_d, sin_d], axis=-1)
    qk = _qk_prep(proj, cos_full, sin_signed, q_norm_o[0].reshape(1, -1), k_norm_o[0].reshape(1, -1), lpad)
    lam_vecs = jnp.stack([lam_q1_o[0], lam_k1_o[0], lam_q2_o[0], lam_k2_o[0]]).astype(F32)
    o = _diff_attn(qk, proj, lam_vecs, diff_norm_o[0].reshape(1, -1), lpad, seq_valid_end, lam_init)
    h = _matmul_res(o.reshape(m, -1), w_out_o[0].astype(BF16), h, keep)
    h = _ffn(h, ffn_norm[1], w_up[1].astype(BF16), conv_w[1], conv_b[1], w_down[1].astype(BF16), keep)

    return h.reshape(nb, lpad, d)[:, CHUNK:CHUNK + s]
```

```python
import functools
import math

import jax
import jax.numpy as jnp
from jax import lax
from jax.experimental import pallas as pl
from jax.experimental.pallas import tpu as pltpu

F32 = jnp.float32
BF16 = jnp.bfloat16

D_MODEL = 2048
CHUNK = 64
N_META = 16
META_OFFSET = CHUNK - N_META
EPS = 1e-6
ROPE_THETA = 10000.0

GLA_HEADS = 4
GLA_DK = 128
GLA_DV = 256
GLA_GATE_RANK = 16
GLA_GATE_NORM = 16.0

RET_HEADS = 4
RET_DK = 256
RET_DV = 256

DIFF_HEADS = 8
DIFF_DK = 128
DIFF_DV = 256

D_FF = 5632
CONV_W = 3

LANES = 128
ATTN_TILE = 256
ROW_TILE = 512
HALO = 16
FF_TILE = 512
VMEM_LIMIT = 56 * 2**20
PAIR_BLOCK = 8
LOG2E = math.log2(math.e)


def _params(semantics):
    return pltpu.CompilerParams(dimension_semantics=semantics, vmem_limit_bytes=VMEM_LIMIT)


def _rms(x, g):
    return x * lax.rsqrt(jnp.mean(x * x, axis=-1, keepdims=True) + EPS) * g


def _sigmoid(x):
    return 1.0 / (1.0 + jnp.exp(-x))


def _norm_matmul_kernel(x_ref, g_ref, w_ref, o_ref, xn_ref):
    @pl.when(pl.program_id(1) == 0)
    def _():
        xn_ref[...] = _rms(x_ref[...], g_ref[...]).astype(BF16)

    o_ref[...] = jnp.dot(xn_ref[...], w_ref[...], preferred_element_type=F32).astype(o_ref.dtype)


def _norm_matmul(x, g, w, tn):
    m, d = x.shape
    n = w.shape[1]
    return pl.pallas_call(
        _norm_matmul_kernel,
        grid=(m // ROW_TILE, n // tn),
        in_specs=[
            pl.BlockSpec((ROW_TILE, d), lambda i, j: (i, 0)),
            pl.BlockSpec((1, d), lambda i, j: (0, 0)),
            pl.BlockSpec((d, tn), lambda i, j: (0, j)),
        ],
        out_specs=pl.BlockSpec((ROW_TILE, tn), lambda i, j: (i, j)),
        out_shape=jax.ShapeDtypeStruct((m, n), BF16),
        scratch_shapes=[pltpu.VMEM((ROW_TILE, d), BF16)],
        compiler_params=_params(("parallel", "arbitrary")),
        name="norm_matmul",
    )(x, g.reshape(1, d), w)


def _norm_matmul_qk_kernel(x_ref, g_ref, w_ref, cos_ref, sin_ref, gain_ref, o_ref, xn_ref):
    @pl.when(pl.program_id(1) == 0)
    def _():
        xn_ref[...] = _rms(x_ref[...], g_ref[...]).astype(BF16)

    y = jnp.dot(xn_ref[...], w_ref[...], preferred_element_type=F32)
    cos = cos_ref[...]
    sin = sin_ref[...]
    gain = gain_ref[...]
    for s in range(y.shape[1] // DIFF_DK):
        sl = slice(s * DIFF_DK, (s + 1) * DIFF_DK)
        ys = _rms(y[:, sl], gain)
        o_ref[:, sl] = (ys * cos + pltpu.roll(ys, DIFF_DK // 2, axis=1) * sin).astype(o_ref.dtype)


def _norm_matmul_qk(x, g, w, cos, sin, gains, tn):
    m, d = x.shape
    n = w.shape[1]
    tiles_per_gain = n // (2 * tn)
    return pl.pallas_call(
        _norm_matmul_qk_kernel,
        grid=(m // ROW_TILE, n // tn),
        in_specs=[
            pl.BlockSpec((ROW_TILE, d), lambda i, j: (i, 0)),
            pl.BlockSpec((1, d), lambda i, j: (0, 0)),
            pl.BlockSpec((d, tn), lambda i, j: (0, j)),
            pl.BlockSpec((ROW_TILE, DIFF_DK), lambda i, j: (i, 0)),
            pl.BlockSpec((ROW_TILE, DIFF_DK), lambda i, j: (i, 0)),
            pl.BlockSpec((None, 1, DIFF_DK), lambda i, j: (j // tiles_per_gain, 0, 0)),
        ],
        out_specs=pl.BlockSpec((ROW_TILE, tn), lambda i, j: (i, j)),
        out_shape=jax.ShapeDtypeStruct((m, n), BF16),
        scratch_shapes=[pltpu.VMEM((ROW_TILE, d), BF16)],
        compiler_params=_params(("parallel", "arbitrary")),
        name="norm_matmul_qk",
    )(x, g.reshape(1, d), w, cos, sin, gains)


def _norm_matmul_t_kernel(x_ref, g_ref, wt_ref, o_ref, xn_ref):
    @pl.when(pl.program_id(1) == 0)
    def _():
        xn_ref[...] = _rms(x_ref[...], g_ref[...]).astype(BF16)

    o_ref[...] = lax.dot_general(wt_ref[...], xn_ref[...], (((1,), (1,)), ((), ())),
                                 preferred_element_type=F32).astype(o_ref.dtype)


def _norm_matmul_t(x, g, wt, tn):
    m, d = x.shape
    n = wt.shape[0]
    return pl.pallas_call(
        _norm_matmul_t_kernel,
        grid=(m // ROW_TILE, n // tn),
        in_specs=[
            pl.BlockSpec((ROW_TILE, d), lambda i, j: (i, 0)),
            pl.BlockSpec((1, d), lambda i, j: (0, 0)),
            pl.BlockSpec((tn, d), lambda i, j: (j, 0)),
        ],
        out_specs=pl.BlockSpec((tn, ROW_TILE), lambda i, j: (j, i)),
        out_shape=jax.ShapeDtypeStruct((n, m), BF16),
        scratch_shapes=[pltpu.VMEM((ROW_TILE, d), BF16)],
        compiler_params=_params(("parallel", "arbitrary")),
        name="norm_matmul_t",
    )(x, g.reshape(1, d), wt)


def _matmul_res_kernel(a_ref, w_ref, h_ref, keep_ref, o_ref):
    mix = jnp.dot(a_ref[...], w_ref[...], preferred_element_type=F32)
    o_ref[...] = h_ref[...] + mix * keep_ref[...]


def _matmul_res(a, w, h, keep):
    m, k = a.shape
    n = w.shape[1]
    return pl.pallas_call(
        _matmul_res_kernel,
        grid=(m // ROW_TILE,),
        in_specs=[
            pl.BlockSpec((ROW_TILE, k), lambda i: (i, 0)),
            pl.BlockSpec((k, n), lambda i: (0, 0)),
            pl.BlockSpec((ROW_TILE, n), lambda i: (i, 0)),
            pl.BlockSpec((ROW_TILE, 1), lambda i: (i, 0)),
        ],
        out_specs=pl.BlockSpec((ROW_TILE, n), lambda i: (i, 0)),
        out_shape=jax.ShapeDtypeStruct((m, n), F32),
        compiler_params=_params(("parallel",)),
        name="matmul_res",
    )(a, w, h, keep)


def _gla_kernel(seq_valid_end, q_ref, k_ref, v_ref, g_ref, lr_ref, w2_ref, b_ref, nrm_ref, sel_ref,
                o_ref, st_ref, cum_ref, qs_ref, ks_ref):
    c = pl.program_id(1)

    @pl.when(c == 0)
    def _():
        st_ref[...] = jnp.zeros_like(st_ref)

    row = c * CHUNK + lax.broadcasted_iota(jnp.int32, (CHUNK, 1), 0)
    vm = jnp.where(row >= META_OFFSET, jnp.where(row < seq_valid_end, 1.0, 0.0), 0.0)
    ii = lax.broadcasted_iota(jnp.int32, (CHUNK, CHUNK), 0)
    jj = lax.broadcasted_iota(jnp.int32, (CHUNK, CHUNK), 1)
    tril = jnp.where(jj <= ii, 1.0, 0.0).astype(F32)

    gate = jnp.dot(lr_ref[...], w2_ref[...], preferred_element_type=F32) + b_ref[...]
    log_a = (jnp.minimum(gate, 0.0) - jnp.log(1.0 + jnp.exp(-jnp.abs(gate)))) * (1.0 / GLA_GATE_NORM) * vm

    for h in range(GLA_HEADS):
        ksl = slice(h * GLA_DK, (h + 1) * GLA_DK)
        cum_ref[h] = jnp.dot(tril, log_a[:, ksl], preferred_element_type=F32,
                             precision=lax.Precision.HIGHEST) * LOG2E
        qs_ref[h] = q_ref[:, ksl].astype(F32) * (GLA_DK ** -0.5)
        ks_ref[h] = k_ref[:, ksl].astype(F32) * vm

    def pair_body(jb, scores):
        sel = sel_ref[jb]
        j0 = jb * PAIR_BLOCK
        new = []
        for h in range(GLA_HEADS):
            cum = cum_ref[h]
            q = qs_ref[h]
            ws = []
            for jj in range(PAIR_BLOCK):
                cj = cum_ref[h, pl.ds(j0 + jj, 1), :]
                kj = ks_ref[h, pl.ds(j0 + jj, 1), :]
                ws.append((jnp.exp2(-jnp.abs(cum - cj)) * (q * kj)).astype(BF16))
            new.append(scores[h] + jnp.dot(jnp.concatenate(ws, axis=1), sel, preferred_element_type=F32))
        return tuple(new)

    scores = lax.fori_loop(0, CHUNK // PAIR_BLOCK, pair_body,
                           tuple(jnp.zeros((CHUNK, CHUNK), F32) for _ in range(GLA_HEADS)))

    for h in range(GLA_HEADS):
        vsl = slice(h * GLA_DV, (h + 1) * GLA_DV)
        cum = cum_ref[h]
        v = v_ref[:, vsl]
        st = st_ref[h]
        q_in = (qs_ref[h] * jnp.exp2(cum)).astype(BF16)
        o = jnp.dot(scores[h].astype(BF16), v, preferred_element_type=F32)
        o = o + lax.dot_general(q_in, st.astype(BF16), (((1,), (1,)), ((), ())), preferred_element_type=F32)
        last = cum[CHUNK - 1:CHUNK, :]
        k_dec = (ks_ref[h] * jnp.exp2(last - cum)).astype(BF16)
        st_ref[h] = st * jnp.exp2(last) + lax.dot_general(v, k_dec, (((0,), (0,)), ((), ())),
                                                          preferred_element_type=F32)
        gv = g_ref[:, vsl].astype(F32)
        o_ref[:, vsl] = (_rms(o, nrm_ref[h]) * (gv * _sigmoid(gv))).astype(o_ref.dtype)


def _gla(proj, lr, w2, b_gate, nrm, sel, lpad, seq_valid_end):
    nb = proj.shape[0]
    hk = GLA_HEADS * GLA_DK
    hv = GLA_HEADS * GLA_DV
    return pl.pallas_call(
        functools.partial(_gla_kernel, seq_valid_end),
        grid=(nb, lpad // CHUNK),
        in_specs=[
            pl.BlockSpec((None, CHUNK, hk), lambda b, c: (b, c, 0)),
            pl.BlockSpec((None, CHUNK, hk), lambda b, c: (b, c, 1)),
            pl.BlockSpec((None, CHUNK, hv), lambda b, c: (b, c, 1)),
            pl.BlockSpec((None, CHUNK, hv), lambda b, c: (b, c, 2)),
            pl.BlockSpec((None, CHUNK, LANES), lambda b, c: (b, c, 0)),
            pl.BlockSpec((LANES, hk), lambda b, c: (0, 0)),
            pl.BlockSpec((1, hk), lambda b, c: (0, 0)),
            pl.BlockSpec((GLA_HEADS, 1, GLA_DV), lambda b, c: (0, 0, 0)),
            pl.BlockSpec((CHUNK // PAIR_BLOCK, PAIR_BLOCK * GLA_DK, CHUNK), lambda b, c: (0, 0, 0)),
        ],
        out_specs=pl.BlockSpec((None, CHUNK, hv), lambda b, c: (b, c, 0)),
        out_shape=jax.ShapeDtypeStruct((nb, lpad, hv), BF16),
        scratch_shapes=[
            pltpu.VMEM((GLA_HEADS, GLA_DV, GLA_DK), F32),
            pltpu.VMEM((GLA_HEADS, CHUNK, GLA_DK), F32),
            pltpu.VMEM((GLA_HEADS, CHUNK, GLA_DK), F32),
            pltpu.VMEM((GLA_HEADS, CHUNK, GLA_DK), F32),
        ],
        compiler_params=_params(("parallel", "arbitrary")),
        name="gla_scan",
    )(proj, proj, proj, proj, lr, w2, b_gate, nrm, sel)


def _ret_kernel(seq_valid_end, q_ref, k_ref, v_ref, g_ref, cos_ref, sin_ref, lg_ref, nrm_ref,
                o_ref, st_ref):
    c = pl.program_id(1)

    @pl.when(c == 0)
    def _():
        st_ref[...] = jnp.zeros_like(st_ref)

    row = c * CHUNK + lax.broadcasted_iota(jnp.int32, (CHUNK, 1), 0)
    vm = jnp.where(row >= META_OFFSET, jnp.where(row < seq_valid_end, 1.0, 0.0), 0.0)
    n_i = lax.broadcasted_iota(jnp.int32, (CHUNK, CHUNK), 0).astype(F32)
    n_j = lax.broadcasted_iota(jnp.int32, (CHUNK, CHUNK), 1).astype(F32)
    dist = jnp.abs(n_i - n_j)
    n_col = lax.broadcasted_iota(jnp.int32, (CHUNK, 1), 0).astype(F32)
    cos = cos_ref[...]
    sin = sin_ref[...]
    half = RET_DK // 2

    def rot(x):
        x1 = x[:, :half]
        x2 = x[:, half:]
        return jnp.concatenate([x1 * cos - x2 * sin, x2 * cos + x1 * sin], axis=-1)

    for h in range(RET_HEADS):
        ksl = slice(h * RET_DK, (h + 1) * RET_DK)
        vsl = slice(h * RET_DV, (h + 1) * RET_DV)
        lg = lg_ref[h]
        intra = jnp.exp(dist * lg)
        xi = jnp.exp((n_col + 1.0) * lg)
        zeta = jnp.exp((CHUNK - 1.0 - n_col) * lg)
        g_chunk = jnp.exp(CHUNK * lg)
        qr = rot(q_ref[:, ksl].astype(F32)).astype(BF16)
        kr = rot(k_ref[:, ksl].astype(F32)) * (RET_DK ** -0.5) * vm
        v = v_ref[:, vsl]
        st = st_ref[h]
        s = lax.dot_general(qr, kr.astype(BF16), (((1,), (1,)), ((), ())), preferred_element_type=F32) * intra
        o = jnp.dot(s.astype(BF16), v, preferred_element_type=F32)
        o = o + lax.dot_general(qr, st.astype(BF16), (((1,), (1,)), ((), ())), preferred_element_type=F32) * xi
        st_ref[h] = st * g_chunk + lax.dot_general(v, (kr * zeta).astype(BF16), (((0,), (0,)), ((), ())),
                                                   preferred_element_type=F32)
        gv = g_ref[:, vsl].astype(F32)
        o_ref[:, vsl] = (_rms(o, nrm_ref[h]) * (gv * _sigmoid(gv))).astype(o_ref.dtype)


def _ret(proj, cos, sin, lg, nrm, lpad, seq_valid_end):
    nb = proj.shape[0]
    hk = RET_HEADS * RET_DK
    hv = RET_HEADS * RET_DV
    base = (2 * GLA_HEADS * GLA_DK + 2 * GLA_HEADS * GLA_DV) // hk
    return pl.pallas_call(
        functools.partial(_ret_kernel, seq_valid_end),
        grid=(nb, lpad // CHUNK),
        in_specs=[
            pl.BlockSpec((None, CHUNK, hk), lambda b, c: (b, c, base)),
            pl.BlockSpec((None, CHUNK, hk), lambda b, c: (b, c, base + 1)),
            pl.BlockSpec((None, CHUNK, hv), lambda b, c: (b, c, base + 2)),
            pl.BlockSpec((None, CHUNK, hv), lambda b, c: (b, c, base + 3)),
            pl.BlockSpec((CHUNK, RET_DK // 2), lambda b, c: (c, 0)),
            pl.BlockSpec((CHUNK, RET_DK // 2), lambda b, c: (c, 0)),
            pl.BlockSpec((RET_HEADS, 1, 1), lambda b, c: (0, 0, 0)),
            pl.BlockSpec((RET_HEADS, 1, RET_DV), lambda b, c: (0, 0, 0)),
        ],
        out_specs=pl.BlockSpec((None, CHUNK, hv), lambda b, c: (b, c, 0)),
        out_shape=jax.ShapeDtypeStruct((nb, lpad, hv), BF16),
        scratch_shapes=[pltpu.VMEM((RET_HEADS, RET_DV, RET_DK), F32)],
        compiler_params=_params(("parallel", "arbitrary")),
        name="ret_scan",
    )(proj, proj, proj, proj, cos, sin, lg, nrm)


def _diff_attn_kernel(seq_valid_end, lam_init, q1_ref, q2_ref, k1_ref, k2_ref, vt_ref, lam_ref, nrm_ref,
                      o_ref, acc1_ref, acc2_ref):
    qi = pl.program_id(2)
    t = ATTN_TILE
    qs = (q1_ref[...], q2_ref[...])
    k_refs = (k1_ref, k2_ref)
    acc_refs = (acc1_ref, acc2_ref)
    q_chunk = lax.shift_right_logical(qi * t + lax.broadcasted_iota(jnp.int32, (1, t), 1), 6)
    nt_dims = (((1,), (1,)), ((), ()))

    def start(j):
        return j * t if isinstance(j, int) else pl.multiple_of(j * t, t)

    def scores(j, masked):
        out = []
        if masked:
            k_idx = j * t + lax.broadcasted_iota(jnp.int32, (t, 1), 0)
            k_chunk = jnp.where(k_idx >= META_OFFSET,
                                jnp.where(k_idx < seq_valid_end, lax.shift_right_logical(k_idx, 6), 2**30), 2**30)
            mask = k_chunk <= q_chunk
        for b in range(2):
            s = lax.dot_general(k_refs[b][pl.ds(start(j), t), :], qs[b], nt_dims, preferred_element_type=F32)
            out.append(jnp.where(mask, s, -1e30) if masked else s)
        return tuple(out)

    def update(j, stats, s_cur):
        vt = vt_ref[:, pl.ds(start(j), t)]
        new = []
        for b in range(2):
            m, l = stats[b]
            s = s_cur[b]
            m_new = jnp.maximum(m, jnp.max(s, axis=0, keepdims=True))
            alpha = jnp.exp2(m - m_new)
            p = jnp.exp2(s - m_new)
            l_new = alpha * l + jnp.sum(p, axis=0, keepdims=True)
            acc_refs[b][...] = alpha * acc_refs[b][...] + jnp.dot(vt, p.astype(BF16), preferred_element_type=F32)
            new.append((m_new, l_new))
        return tuple(new)

    def make_body(masked):
        def body(j, carry):
            stats, s_cur = carry
            s_next = scores(j + 1, masked)
            return update(j, stats, s_cur), s_next
        return body

    for b in range(2):
        acc_refs[b][...] = jnp.zeros_like(acc_refs[b])
    init = (jnp.full((1, t), -1e30, F32), jnp.zeros((1, t), F32))
    carry = ((init, init), scores(0, True))
    carry = lax.fori_loop(0, qi - 1, make_body(False), carry)
    carry = lax.fori_loop(jnp.maximum(qi - 1, 0), qi, make_body(True), carry)
    (_, l1), (_, l2) = update(qi, *carry)

    lam_v = lam_ref[...]
    lam = (jnp.exp(jnp.sum(lam_v[0:1] * lam_v[1:2], axis=-1, keepdims=True))
           - jnp.exp(jnp.sum(lam_v[2:3] * lam_v[3:4], axis=-1, keepdims=True)) + lam_init)
    o_t = acc1_ref[...] * (1.0 / l1) - acc2_ref[...] * (lam * (1.0 / l2))
    o_ref[...] = (_rms(o_t.T, nrm_ref[...]) * (1.0 - lam_init)).astype(o_ref.dtype)


def _diff_attn(qk, vt, lam_vecs, nrm, lpad, seq_valid_end, lam_init):
    nb = qk.shape[0]
    t = ATTN_TILE
    kb = 2 * DIFF_HEADS
    return pl.pallas_call(
        functools.partial(_diff_attn_kernel, seq_valid_end, lam_init),
        grid=(nb, DIFF_HEADS, lpad // t),
        in_specs=[
            pl.BlockSpec((None, t, DIFF_DK), lambda b, h, i: (b, i, 2 * h)),
            pl.BlockSpec((None, t, DIFF_DK), lambda b, h, i: (b, i, 2 * h + 1)),
            pl.BlockSpec((None, lpad, DIFF_DK), lambda b, h, i: (b, 0, kb + 2 * h)),
            pl.BlockSpec((None, lpad, DIFF_DK), lambda b, h, i: (b, 0, kb + 2 * h + 1)),
            pl.BlockSpec((DIFF_DV, lpad), lambda b, h, i: (h, b)),
            pl.BlockSpec((4, DIFF_DK), lambda b, h, i: (0, 0)),
            pl.BlockSpec((1, DIFF_DV), lambda b, h, i: (0, 0)),
        ],
        out_specs=pl.BlockSpec((None, t, DIFF_DV), lambda b, h, i: (b, i, h)),
        out_shape=jax.ShapeDtypeStruct((nb, lpad, DIFF_HEADS * DIFF_DV), BF16),
        scratch_shapes=[pltpu.VMEM((DIFF_DV, t), F32), pltpu.VMEM((DIFF_DV, t), F32)],
        compiler_params=_params(("parallel", "parallel", "arbitrary")),
        name="diff_attn",
    )(qk, qk, qk, qk, vt, lam_vecs, nrm)


def _ffn_kernel(x_ref, halo_ref, g_ref, wv_ref, wg_ref, cwv_ref, cwg_ref, cbv_ref, cbg_ref, wd_ref, keep_ref,
                o_ref, hn_ref, acc_ref):
    i = pl.program_id(0)
    j = pl.program_id(1)
    tm = x_ref.shape[0]

    @pl.when(j == 0)
    def _():
        g = g_ref[...]
        hn_ref[HALO:, :] = _rms(x_ref[...], g).astype(BF16)
        first = jnp.where(i > 0, 1.0, 0.0)
        hn_ref[:HALO, :] = (_rms(halo_ref[...], g) * first).astype(BF16)
        acc_ref[...] = jnp.zeros_like(acc_ref)

    hn = hn_ref[...]

    def conv_branch(w_ref, cw_ref, cb_ref):
        u = jnp.dot(hn, w_ref[...], preferred_element_type=F32)
        cw = cw_ref[...]
        c = cb_ref[...] + cw[0:1] * u[HALO - 2:HALO - 2 + tm]
        c = c + cw[1:2] * u[HALO - 1:HALO - 1 + tm]
        return c + cw[2:3] * u[HALO:HALO + tm]

    val = conv_branch(wv_ref, cwv_ref, cbv_ref)
    gate = conv_branch(wg_ref, cwg_ref, cbg_ref)
    act = (val * (gate * _sigmoid(gate))).astype(BF16)
    acc_ref[...] += jnp.dot(act, wd_ref[...], preferred_element_type=F32)

    @pl.when(j == pl.num_programs(1) - 1)
    def _():
        o_ref[...] = x_ref[...] + acc_ref[...] * keep_ref[...]


def _ffn(x, g, w_up, conv_w, conv_b, w_down, keep):
    m, d = x.shape
    f = w_down.shape[0]
    nf = f // FF_TILE
    tm = ROW_TILE
    halo_blocks = tm // HALO
    return pl.pallas_call(
        _ffn_kernel,
        grid=(m // tm, nf),
        in_specs=[
            pl.BlockSpec((tm, d), lambda i, j: (i, 0)),
            pl.BlockSpec((None, HALO, d), lambda i, j: (jnp.maximum(i * halo_blocks - 1, 0), 0, 0)),
            pl.BlockSpec((1, d), lambda i, j: (0, 0)),
            pl.BlockSpec((d, FF_TILE), lambda i, j: (0, j)),
            pl.BlockSpec((d, FF_TILE), lambda i, j: (0, nf + j)),
            pl.BlockSpec((CONV_W, FF_TILE), lambda i, j: (0, j)),
            pl.BlockSpec((CONV_W, FF_TILE), lambda i, j: (0, nf + j)),
            pl.BlockSpec((1, FF_TILE), lambda i, j: (0, j)),
            pl.BlockSpec((1, FF_TILE), lambda i, j: (0, nf + j)),
            pl.BlockSpec((FF_TILE, d), lambda i, j: (j, 0)),
            pl.BlockSpec((tm, 1), lambda i, j: (i, 0)),
        ],
        out_specs=pl.BlockSpec((tm, d), lambda i, j: (i, 0)),
        out_shape=jax.ShapeDtypeStruct((m, d), F32),
        scratch_shapes=[pltpu.VMEM((tm + HALO, d), BF16), pltpu.VMEM((tm, d), F32)],
        compiler_params=_params(("parallel", "arbitrary")),
        name="conv_ffn",
    )(x, x.reshape(m // HALO, HALO, d), g.reshape(1, d), w_up, w_up, conv_w, conv_w,
      conv_b.reshape(1, -1), conv_b.reshape(1, -1), w_down, keep)


def _rope_tables(pos, inv_freq):
    ang = pos.astype(F32)[:, None] * inv_freq[None, :]
    return jnp.cos(ang), jnp.sin(ang)


def kernel(x, meta, mix_norm_e, w_in_e, gla_w_gate_e, gla_b_gate_e, gla_norm_e, ret_norm_e, w_out_e, mix_norm_o, w_in_o, q_norm_o, k_norm_o, lam_q1_o, lam_k1_o, lam_q2_o, lam_k2_o, diff_norm_o, w_out_o, ffn_norm, w_up, conv_w, conv_b, w_down):
    nb, s, d = x.shape
    seq_valid_end = CHUNK + s
    lpad = -(-seq_valid_end // ATTN_TILE) * ATTN_TILE
    m = nb * lpad

    h = jnp.concatenate([jnp.zeros((nb, META_OFFSET, d), x.dtype),
                         jnp.broadcast_to(meta.astype(x.dtype)[None], (nb, N_META, d)),
                         x,
                         jnp.zeros((nb, lpad - seq_valid_end, d), x.dtype)], axis=1).reshape(m, d)
    idx = jnp.arange(lpad)
    valid = (idx >= META_OFFSET) & (idx < seq_valid_end)
    pos = idx - META_OFFSET
    keep = jnp.tile(valid.astype(F32), nb).reshape(m, 1)

    gla_cols = 2 * GLA_HEADS * GLA_DK + 2 * GLA_HEADS * GLA_DV
    w_in = w_in_e[0]
    w_main = jnp.concatenate([w_in[:, :gla_cols], w_in[:, gla_cols + GLA_GATE_RANK:]], axis=1).astype(BF16)
    w_lr = jnp.pad(w_in[:, gla_cols:gla_cols + GLA_GATE_RANK], ((0, 0), (0, LANES - GLA_GATE_RANK))).astype(BF16)
    w_gate2 = jnp.pad(gla_w_gate_e[0], ((0, LANES - GLA_GATE_RANK), (0, 0))).astype(BF16)
    proj = _norm_matmul(h, mix_norm_e[0], w_main, 1024).reshape(nb, lpad, -1)
    lr = _norm_matmul(h, mix_norm_e[0], w_lr, LANES).reshape(nb, lpad, LANES)
    sel = (jnp.arange(CHUNK)[:, None, None] == jnp.arange(CHUNK)[None, None, :]).astype(BF16)
    sel = jnp.broadcast_to(sel, (CHUNK, GLA_DK, CHUNK)).reshape(CHUNK // PAIR_BLOCK, PAIR_BLOCK * GLA_DK, CHUNK)
    o_a = _gla(proj, lr, w_gate2, gla_b_gate_e[0].reshape(1, -1), gla_norm_e[0].reshape(GLA_HEADS, 1, GLA_DV),
               sel, lpad, seq_valid_end)
    inv_freq_ret = 1.0 / (ROPE_THETA ** jnp.linspace(0.0, 1.0, RET_DK // 2, dtype=F32))
    cos_r, sin_r = _rope_tables(pos, inv_freq_ret)
    log_gamma = jnp.log(1.0 - 2.0 ** (-5.0 - jnp.arange(RET_HEADS, dtype=F32))).reshape(RET_HEADS, 1, 1)
    o_b = _ret(proj, cos_r, sin_r, log_gamma, ret_norm_e[0].reshape(RET_HEADS, 1, RET_DV), lpad, seq_valid_end)
    mix_in = jnp.concatenate([o_a, o_b], axis=-1).reshape(m, -1)
    h = _matmul_res(mix_in, w_out_e[0].astype(BF16), h, keep)
    h = _ffn(h, ffn_norm[0], w_up[0].astype(BF16), conv_w[0], conv_b[0], w_down[0].astype(BF16), keep)

    layer = 1
    lam_init = 0.8 - 0.6 * math.exp(-0.3 * layer)
    qk_cols = 4 * DIFF_HEADS * DIFF_DK
    inv_freq = 1.0 / (ROPE_THETA ** (jnp.arange(0, DIFF_DK, 2, dtype=F32) / DIFF_DK))
    cos_d, sin_d = _rope_tables(pos, inv_freq)
    cos_full = jnp.tile(jnp.concatenate([cos_d, cos_d], axis=-1), (nb, 1))
    sin_signed = jnp.tile(jnp.concatenate([-sin_d, sin_d], axis=-1), (nb, 1))
    gains = jnp.stack([q_norm_o[0] * (DIFF_DK ** -0.5 * LOG2E), k_norm_o[0]]).reshape(2, 1, DIFF_DK)
    qk = _norm_matmul_qk(h, mix_norm_o[0], w_in_o[0][:, :qk_cols].astype(BF16), cos_full, sin_signed, gains,
                         1024).reshape(nb, lpad, qk_cols)
    vt = _norm_matmul_t(h, mix_norm_o[0], w_in_o[0][:, qk_cols:].T.astype(BF16), 1024)
    lam_vecs = jnp.stack([lam_q1_o[0], lam_k1_o[0], lam_q2_o[0], lam_k2_o[0]]).astype(F32)
    o = _diff_attn(qk, vt, lam_vecs, diff_norm_o[0].reshape(1, -1), lpad, seq_valid_end, lam_init)
    h = _matmul_res(o.reshape(m, -1), w_out_o[0].astype(BF16), h, keep)
    h = _ffn(h, ffn_norm[1], w_up[1].astype(BF16), conv_w[1], conv_b[1], w_down[1].astype(BF16), keep)

    return h.reshape(nb, lpad, d)[:, CHUNK:CHUNK + s]
```

```python
import functools
import math

import jax
import jax.numpy as jnp
from jax import lax
from jax.experimental import pallas as pl
from jax.experimental.pallas import tpu as pltpu

F32 = jnp.float32
BF16 = jnp.bfloat16

D_MODEL = 2048
CHUNK = 64
N_META = 16
META_OFFSET = CHUNK - N_META
EPS = 1e-6
ROPE_THETA = 10000.0

GLA_HEADS = 4
GLA_DK = 128
GLA_DV = 256
GLA_GATE_RANK = 16
GLA_GATE_NORM = 16.0

RET_HEADS = 4
RET_DK = 256
RET_DV = 256

DIFF_HEADS = 8
DIFF_DK = 128
DIFF_DV = 256

D_FF = 5632
CONV_W = 3

LANES = 128
ATTN_TILE = 256
ROW_TILE = 512
PROJ_TILE = 512
HALO = 16
FF_TILE = 512
VMEM_LIMIT = 56 * 2**20
PAIR_BLOCK = 8
LOG2E = math.log2(math.e)


def _params(semantics):
    return pltpu.CompilerParams(dimension_semantics=semantics, vmem_limit_bytes=VMEM_LIMIT)


def _rms(x, g):
    return x * lax.rsqrt(jnp.mean(x * x, axis=-1, keepdims=True) + EPS) * g


def _sigmoid(x):
    return 1.0 / (1.0 + jnp.exp(-x))


def _norm_matmul_kernel(x_ref, g_ref, w_ref, o_ref, xn_ref):
    @pl.when(pl.program_id(1) == 0)
    def _():
        xn_ref[...] = _rms(x_ref[...], g_ref[...]).astype(BF16)

    o_ref[...] = jnp.dot(xn_ref[...], w_ref[...], preferred_element_type=F32).astype(o_ref.dtype)


def _norm_matmul(x, g, w, tn):
    m, d = x.shape
    n = w.shape[1]
    return pl.pallas_call(
        _norm_matmul_kernel,
        grid=(m // PROJ_TILE, n // tn),
        in_specs=[
            pl.BlockSpec((PROJ_TILE, d), lambda i, j: (i, 0)),
            pl.BlockSpec((1, d), lambda i, j: (0, 0)),
            pl.BlockSpec((d, tn), lambda i, j: (0, j)),
        ],
        out_specs=pl.BlockSpec((PROJ_TILE, tn), lambda i, j: (i, j)),
        out_shape=jax.ShapeDtypeStruct((m, n), BF16),
        scratch_shapes=[pltpu.VMEM((PROJ_TILE, d), BF16)],
        compiler_params=_params(("parallel", "arbitrary")),
        name="norm_matmul",
    )(x, g.reshape(1, d), w)


def _norm_matmul_qk_kernel(x_ref, g_ref, w_ref, cos_ref, sin_ref, gain_ref, o_ref, xn_ref):
    @pl.when(pl.program_id(1) == 0)
    def _():
        xn_ref[...] = _rms(x_ref[...], g_ref[...]).astype(BF16)

    y = jnp.dot(xn_ref[...], w_ref[...], preferred_element_type=F32)
    cos = cos_ref[...]
    sin = sin_ref[...]
    gain = gain_ref[...]
    for s in range(y.shape[1] // DIFF_DK):
        sl = slice(s * DIFF_DK, (s + 1) * DIFF_DK)
        ys = _rms(y[:, sl], gain)
        o_ref[:, sl] = (ys * cos + pltpu.roll(ys, DIFF_DK // 2, axis=1) * sin).astype(o_ref.dtype)


def _norm_matmul_qk(x, g, w, cos, sin, gains, tn):
    m, d = x.shape
    n = w.shape[1]
    tiles_per_gain = n // (2 * tn)
    return pl.pallas_call(
        _norm_matmul_qk_kernel,
        grid=(m // PROJ_TILE, n // tn),
        in_specs=[
            pl.BlockSpec((PROJ_TILE, d), lambda i, j: (i, 0)),
            pl.BlockSpec((1, d), lambda i, j: (0, 0)),
            pl.BlockSpec((d, tn), lambda i, j: (0, j)),
            pl.BlockSpec((PROJ_TILE, DIFF_DK), lambda i, j: (i, 0)),
            pl.BlockSpec((PROJ_TILE, DIFF_DK), lambda i, j: (i, 0)),
            pl.BlockSpec((None, 1, DIFF_DK), lambda i, j: (j // tiles_per_gain, 0, 0)),
        ],
        out_specs=pl.BlockSpec((PROJ_TILE, tn), lambda i, j: (i, j)),
        out_shape=jax.ShapeDtypeStruct((m, n), BF16),
        scratch_shapes=[pltpu.VMEM((PROJ_TILE, d), BF16)],
        compiler_params=_params(("parallel", "arbitrary")),
        name="norm_matmul_qk",
    )(x, g.reshape(1, d), w, cos, sin, gains)


def _norm_matmul_t_kernel(x_ref, g_ref, wt_ref, o_ref, xn_ref):
    @pl.when(pl.program_id(1) == 0)
    def _():
        xn_ref[...] = _rms(x_ref[...], g_ref[...]).astype(BF16)

    o_ref[...] = lax.dot_general(wt_ref[...], xn_ref[...], (((1,), (1,)), ((), ())),
                                 preferred_element_type=F32).astype(o_ref.dtype)


def _norm_matmul_t(x, g, wt, tn):
    m, d = x.shape
    n = wt.shape[0]
    return pl.pallas_call(
        _norm_matmul_t_kernel,
        grid=(m // PROJ_TILE, n // tn),
        in_specs=[
            pl.BlockSpec((PROJ_TILE, d), lambda i, j: (i, 0)),
            pl.BlockSpec((1, d), lambda i, j: (0, 0)),
            pl.BlockSpec((tn, d), lambda i, j: (j, 0)),
        ],
        out_specs=pl.BlockSpec((tn, PROJ_TILE), lambda i, j: (j, i)),
        out_shape=jax.ShapeDtypeStruct((n, m), BF16),
        scratch_shapes=[pltpu.VMEM((PROJ_TILE, d), BF16)],
        compiler_params=_params(("parallel", "arbitrary")),
        name="norm_matmul_t",
    )(x, g.reshape(1, d), wt)


def _matmul_res_kernel(n_in, *refs):
    a_refs = refs[:n_in]
    w_ref, h_ref, keep_ref, o_ref = refs[n_in:]
    mix = None
    k0 = 0
    for a_ref in a_refs:
        k1 = k0 + a_ref.shape[1]
        part = jnp.dot(a_ref[...], w_ref[k0:k1, :], preferred_element_type=F32)
        mix = part if mix is None else mix + part
        k0 = k1
    o_ref[...] = h_ref[...] + mix * keep_ref[...]


def _matmul_res(a_parts, w, h, keep):
    m = h.shape[0]
    k, n = w.shape
    return pl.pallas_call(
        functools.partial(_matmul_res_kernel, len(a_parts)),
        grid=(m // ROW_TILE,),
        in_specs=[pl.BlockSpec((ROW_TILE, a.shape[1]), lambda i: (i, 0)) for a in a_parts] + [
            pl.BlockSpec((k, n), lambda i: (0, 0)),
            pl.BlockSpec((ROW_TILE, n), lambda i: (i, 0)),
            pl.BlockSpec((ROW_TILE, 1), lambda i: (i, 0)),
        ],
        out_specs=pl.BlockSpec((ROW_TILE, n), lambda i: (i, 0)),
        out_shape=jax.ShapeDtypeStruct((m, n), F32),
        compiler_params=_params(("parallel",)),
        name="matmul_res",
    )(*a_parts, w, h, keep)


def _gla_kernel(seq_valid_end, q_ref, k_ref, v_ref, g_ref, lr_ref, w2_ref, b_ref, nrm_ref, sel_ref,
                o_ref, st_ref, cum_ref, qs_ref, ks_ref):
    c = pl.program_id(1)

    @pl.when(c == 0)
    def _():
        st_ref[...] = jnp.zeros_like(st_ref)

    row = c * CHUNK + lax.broadcasted_iota(jnp.int32, (CHUNK, 1), 0)
    vm = jnp.where(row >= META_OFFSET, jnp.where(row < seq_valid_end, 1.0, 0.0), 0.0)
    ii = lax.broadcasted_iota(jnp.int32, (CHUNK, CHUNK), 0)
    jj = lax.broadcasted_iota(jnp.int32, (CHUNK, CHUNK), 1)
    tril = jnp.where(jj <= ii, 1.0, 0.0).astype(F32)

    gate = jnp.dot(lr_ref[...], w2_ref[...], preferred_element_type=F32) + b_ref[...]
    log_a = (jnp.minimum(gate, 0.0) - jnp.log(1.0 + jnp.exp(-jnp.abs(gate)))) * (1.0 / GLA_GATE_NORM) * vm

    for h in range(GLA_HEADS):
        ksl = slice(h * GLA_DK, (h + 1) * GLA_DK)
        cum_ref[h] = jnp.dot(tril, log_a[:, ksl], preferred_element_type=F32,
                             precision=lax.Precision.HIGHEST) * LOG2E
        qs_ref[h] = q_ref[:, ksl].astype(F32) * (GLA_DK ** -0.5)
        ks_ref[h] = k_ref[:, ksl].astype(F32) * vm

    def decay(cum, cj, r0):
        r1 = r0 + PAIR_BLOCK
        parts = [jnp.exp2(-jnp.abs(cum[r0:r1] - cj))]
        if r0 > 0:
            parts.insert(0, jnp.exp2(cj - cum[:r0]))
        if r1 < CHUNK:
            parts.append(jnp.exp2(cum[r1:] - cj))
        return jnp.concatenate(parts, axis=0)

    scores = []
    for h in range(GLA_HEADS):
        cum = cum_ref[h]
        q = qs_ref[h]
        total = None
        for jb in range(CHUNK // PAIR_BLOCK):
            ws = []
            for j in range(jb * PAIR_BLOCK, (jb + 1) * PAIR_BLOCK):
                cj = cum_ref[h, j:j + 1, :]
                kj = ks_ref[h, j:j + 1, :]
                ws.append((decay(cum, cj, jb * PAIR_BLOCK) * (q * kj)).astype(BF16))
            part = jnp.dot(jnp.concatenate(ws, axis=1), sel_ref[jb], preferred_element_type=F32)
            total = part if total is None else total + part
        scores.append(total)

    for h in range(GLA_HEADS):
        vsl = slice(h * GLA_DV, (h + 1) * GLA_DV)
        cum = cum_ref[h]
        v = v_ref[:, vsl]
        st = st_ref[h]
        q_in = (qs_ref[h] * jnp.exp2(cum)).astype(BF16)
        o = jnp.dot(scores[h].astype(BF16), v, preferred_element_type=F32)
        o = o + lax.dot_general(q_in, st.astype(BF16), (((1,), (1,)), ((), ())), preferred_element_type=F32)
        last = cum[CHUNK - 1:CHUNK, :]
        k_dec = (ks_ref[h] * jnp.exp2(last - cum)).astype(BF16)
        st_ref[h] = st * jnp.exp2(last) + lax.dot_general(v, k_dec, (((0,), (0,)), ((), ())),
                                                          preferred_element_type=F32)
        gv = g_ref[:, vsl].astype(F32)
        o_ref[:, vsl] = (_rms(o, nrm_ref[h]) * (gv * _sigmoid(gv))).astype(o_ref.dtype)


def _gla(proj, lr, w2, b_gate, nrm, sel, lpad, seq_valid_end):
    nb = proj.shape[0]
    hk = GLA_HEADS * GLA_DK
    hv = GLA_HEADS * GLA_DV
    return pl.pallas_call(
        functools.partial(_gla_kernel, seq_valid_end),
        grid=(nb, lpad // CHUNK),
        in_specs=[
            pl.BlockSpec((None, CHUNK, hk), lambda b, c: (b, c, 0)),
            pl.BlockSpec((None, CHUNK, hk), lambda b, c: (b, c, 1)),
            pl.BlockSpec((None, CHUNK, hv), lambda b, c: (b, c, 1)),
            pl.BlockSpec((None, CHUNK, hv), lambda b, c: (b, c, 2)),
            pl.BlockSpec((None, CHUNK, LANES), lambda b, c: (b, c, 0)),
            pl.BlockSpec((LANES, hk), lambda b, c: (0, 0)),
            pl.BlockSpec((1, hk), lambda b, c: (0, 0)),
            pl.BlockSpec((GLA_HEADS, 1, GLA_DV), lambda b, c: (0, 0, 0)),
            pl.BlockSpec((CHUNK // PAIR_BLOCK, PAIR_BLOCK * GLA_DK, CHUNK), lambda b, c: (0, 0, 0)),
        ],
        out_specs=pl.BlockSpec((None, CHUNK, hv), lambda b, c: (b, c, 0)),
        out_shape=jax.ShapeDtypeStruct((nb, lpad, hv), BF16),
        scratch_shapes=[
            pltpu.VMEM((GLA_HEADS, GLA_DV, GLA_DK), F32),
            pltpu.VMEM((GLA_HEADS, CHUNK, GLA_DK), F32),
            pltpu.VMEM((GLA_HEADS, CHUNK, GLA_DK), F32),
            pltpu.VMEM((GLA_HEADS, CHUNK, GLA_DK), F32),
        ],
        compiler_params=_params(("parallel", "arbitrary")),
        name="gla_scan",
    )(proj, proj, proj, proj, lr, w2, b_gate, nrm, sel)


def _ret_kernel(seq_valid_end, q_ref, k_ref, v_ref, g_ref, cos_ref, sin_ref, lg_ref, nrm_ref,
                o_ref, st_ref):
    c = pl.program_id(1)

    @pl.when(c == 0)
    def _():
        st_ref[...] = jnp.zeros_like(st_ref)

    row = c * CHUNK + lax.broadcasted_iota(jnp.int32, (CHUNK, 1), 0)
    vm = jnp.where(row >= META_OFFSET, jnp.where(row < seq_valid_end, 1.0, 0.0), 0.0)
    n_i = lax.broadcasted_iota(jnp.int32, (CHUNK, CHUNK), 0).astype(F32)
    n_j = lax.broadcasted_iota(jnp.int32, (CHUNK, CHUNK), 1).astype(F32)
    dist = jnp.abs(n_i - n_j)
    n_col = lax.broadcasted_iota(jnp.int32, (CHUNK, 1), 0).astype(F32)
    cos = cos_ref[...]
    sin = sin_ref[...]
    half = RET_DK // 2

    def rot(x):
        x1 = x[:, :half]
        x2 = x[:, half:]
        return jnp.concatenate([x1 * cos - x2 * sin, x2 * cos + x1 * sin], axis=-1)

    for h in range(RET_HEADS):
        ksl = slice(h * RET_DK, (h + 1) * RET_DK)
        vsl = slice(h * RET_DV, (h + 1) * RET_DV)
        lg = lg_ref[h]
        intra = jnp.exp(dist * lg)
        xi = jnp.exp((n_col + 1.0) * lg)
        zeta = jnp.exp((CHUNK - 1.0 - n_col) * lg)
        g_chunk = jnp.exp(CHUNK * lg)
        qr = rot(q_ref[:, ksl].astype(F32)).astype(BF16)
        kr = rot(k_ref[:, ksl].astype(F32)) * (RET_DK ** -0.5) * vm
        v = v_ref[:, vsl]
        st = st_ref[h]
        s = lax.dot_general(qr, kr.astype(BF16), (((1,), (1,)), ((), ())), preferred_element_type=F32) * intra
        o = jnp.dot(s.astype(BF16), v, preferred_element_type=F32)
        o = o + lax.dot_general(qr, st.astype(BF16), (((1,), (1,)), ((), ())), preferred_element_type=F32) * xi
        st_ref[h] = st * g_chunk + lax.dot_general(v, (kr * zeta).astype(BF16), (((0,), (0,)), ((), ())),
                                                   preferred_element_type=F32)
        gv = g_ref[:, vsl].astype(F32)
        o_ref[:, vsl] = (_rms(o, nrm_ref[h]) * (gv * _sigmoid(gv))).astype(o_ref.dtype)


def _ret(proj, cos, sin, lg, nrm, lpad, seq_valid_end):
    nb = proj.shape[0]
    hk = RET_HEADS * RET_DK
    hv = RET_HEADS * RET_DV
    base = (2 * GLA_HEADS * GLA_DK + 2 * GLA_HEADS * GLA_DV) // hk
    return pl.pallas_call(
        functools.partial(_ret_kernel, seq_valid_end),
        grid=(nb, lpad // CHUNK),
        in_specs=[
            pl.BlockSpec((None, CHUNK, hk), lambda b, c: (b, c, base)),
            pl.BlockSpec((None, CHUNK, hk), lambda b, c: (b, c, base + 1)),
            pl.BlockSpec((None, CHUNK, hv), lambda b, c: (b, c, base + 2)),
            pl.BlockSpec((None, CHUNK, hv), lambda b, c: (b, c, base + 3)),
            pl.BlockSpec((CHUNK, RET_DK // 2), lambda b, c: (c, 0)),
            pl.BlockSpec((CHUNK, RET_DK // 2), lambda b, c: (c, 0)),
            pl.BlockSpec((RET_HEADS, 1, 1), lambda b, c: (0, 0, 0)),
            pl.BlockSpec((RET_HEADS, 1, RET_DV), lambda b, c: (0, 0, 0)),
        ],
        out_specs=pl.BlockSpec((None, CHUNK, hv), lambda b, c: (b, c, 0)),
        out_shape=jax.ShapeDtypeStruct((nb, lpad, hv), BF16),
        scratch_shapes=[pltpu.VMEM((RET_HEADS, RET_DV, RET_DK), F32)],
        compiler_params=_params(("parallel", "arbitrary")),
        name="ret_scan",
    )(proj, proj, proj, proj, cos, sin, lg, nrm)


def _diff_attn_kernel(seq_valid_end, lam_init, q1_ref, q2_ref, k1_ref, k2_ref, vt_ref, lam_ref, nrm_ref,
                      o_ref, acc1_ref, acc2_ref):
    qi = pl.program_id(2)
    t = ATTN_TILE
    qs = (q1_ref[...], q2_ref[...])
    k_refs = (k1_ref, k2_ref)
    acc_refs = (acc1_ref, acc2_ref)
    q_chunk = lax.shift_right_logical(qi * t + lax.broadcasted_iota(jnp.int32, (1, t), 1), 6)
    nt_dims = (((1,), (1,)), ((), ()))

    def start(j):
        return j * t if isinstance(j, int) else pl.multiple_of(j * t, t)

    def scores(j, masked):
        out = []
        if masked:
            k_idx = j * t + lax.broadcasted_iota(jnp.int32, (t, 1), 0)
            k_chunk = jnp.where(k_idx >= META_OFFSET,
                                jnp.where(k_idx < seq_valid_end, lax.shift_right_logical(k_idx, 6), 2**30), 2**30)
            mask = k_chunk <= q_chunk
        for b in range(2):
            s = lax.dot_general(k_refs[b][pl.ds(start(j), t), :], qs[b], nt_dims, preferred_element_type=F32)
            out.append(jnp.where(mask, s, -1e30) if masked else s)
        return tuple(out)

    def update(j, stats, s_cur):
        vt = vt_ref[:, pl.ds(start(j), t)]
        new = []
        for b in range(2):
            m, l = stats[b]
            s = s_cur[b]
            m_new = jnp.maximum(m, jnp.max(s, axis=0, keepdims=True))
            alpha = jnp.exp2(m - m_new)
            p = jnp.exp2(s - m_new)
            l_new = alpha * l + jnp.sum(p, axis=0, keepdims=True)
            acc_refs[b][...] = alpha * acc_refs[b][...] + jnp.dot(vt, p.astype(BF16), preferred_element_type=F32)
            new.append((m_new, l_new))
        return tuple(new)

    def make_body(masked):
        def body(j, carry):
            stats, s_cur = carry
            s_next = scores(j + 1, masked)
            return update(j, stats, s_cur), s_next
        return body

    def pair_body(p, carry):
        stats, s_cur = carry
        j = 2 * p
        s_mid = scores(j + 1, False)
        stats = update(j, stats, s_cur)
        s_next = scores(j + 2, False)
        return update(j + 1, stats, s_mid), s_next

    for b in range(2):
        acc_refs[b][...] = jnp.zeros_like(acc_refs[b])
    init = (jnp.full((1, t), -1e30, F32), jnp.zeros((1, t), F32))
    carry = ((init, init), scores(0, True))
    n_plain = jnp.maximum(qi - 1, 0)
    n_pairs = lax.shift_right_logical(n_plain, 1)
    carry = lax.fori_loop(0, n_pairs, pair_body, carry)
    carry = lax.fori_loop(2 * n_pairs, n_plain, make_body(False), carry)
    carry = lax.fori_loop(n_plain, qi, make_body(True), carry)
    (_, l1), (_, l2) = update(qi, *carry)

    lam_v = lam_ref[...]
    lam = (jnp.exp(jnp.sum(lam_v[0:1] * lam_v[1:2], axis=-1, keepdims=True))
           - jnp.exp(jnp.sum(lam_v[2:3] * lam_v[3:4], axis=-1, keepdims=True)) + lam_init)
    o_t = acc1_ref[...] * (1.0 / l1) - acc2_ref[...] * (lam * (1.0 / l2))
    o_ref[...] = (_rms(o_t.T, nrm_ref[...]) * (1.0 - lam_init)).astype(o_ref.dtype)


def _diff_attn(qk, vt, lam_vecs, nrm, lpad, seq_valid_end, lam_init):
    nb = qk.shape[0]
    t = ATTN_TILE
    kb = 2 * DIFF_HEADS
    return pl.pallas_call(
        functools.partial(_diff_attn_kernel, seq_valid_end, lam_init),
        grid=(nb, DIFF_HEADS, lpad // t),
        in_specs=[
            pl.BlockSpec((None, t, DIFF_DK), lambda b, h, i: (b, i, 2 * h)),
            pl.BlockSpec((None, t, DIFF_DK), lambda b, h, i: (b, i, 2 * h + 1)),
            pl.BlockSpec((None, lpad, DIFF_DK), lambda b, h, i: (b, 0, kb + 2 * h)),
            pl.BlockSpec((None, lpad, DIFF_DK), lambda b, h, i: (b, 0, kb + 2 * h + 1)),
            pl.BlockSpec((DIFF_DV, lpad), lambda b, h, i: (h, b)),
            pl.BlockSpec((4, DIFF_DK), lambda b, h, i: (0, 0)),
            pl.BlockSpec((1, DIFF_DV), lambda b, h, i: (0, 0)),
        ],
        out_specs=pl.BlockSpec((None, t, DIFF_DV), lambda b, h, i: (b, i, h)),
        out_shape=jax.ShapeDtypeStruct((nb, lpad, DIFF_HEADS * DIFF_DV), BF16),
        scratch_shapes=[pltpu.VMEM((DIFF_DV, t), F32), pltpu.VMEM((DIFF_DV, t), F32)],
        compiler_params=_params(("parallel", "parallel", "arbitrary")),
        name="diff_attn",
    )(qk, qk, qk, qk, vt, lam_vecs, nrm)


def _ffn_kernel(x_ref, halo_ref, g_ref, wv_ref, wg_ref, cwv_ref, cwg_ref, cbv_ref, cbg_ref, wd_ref, keep_ref,
                o_ref, hn_ref, acc_ref):
    i = pl.program_id(0)
    j = pl.program_id(1)
    tm = x_ref.shape[0]

    @pl.when(j == 0)
    def _():
        g = g_ref[...]
        hn_ref[HALO:, :] = _rms(x_ref[...], g).astype(BF16)
        first = jnp.where(i > 0, 1.0, 0.0)
        hn_ref[:HALO, :] = (_rms(halo_ref[...], g) * first).astype(BF16)
        acc_ref[...] = jnp.zeros_like(acc_ref)

    hn = hn_ref[...]

    def conv_branch(w_ref, cw_ref, cb_ref):
        u = jnp.dot(hn, w_ref[...], preferred_element_type=F32)
        cw = cw_ref[...]
        c = cb_ref[...] + cw[0:1] * u[HALO - 2:HALO - 2 + tm]
        c = c + cw[1:2] * u[HALO - 1:HALO - 1 + tm]
        return c + cw[2:3] * u[HALO:HALO + tm]

    val = conv_branch(wv_ref, cwv_ref, cbv_ref)
    gate = conv_branch(wg_ref, cwg_ref, cbg_ref)
    act = (val * (gate * _sigmoid(gate))).astype(BF16)
    acc_ref[...] += jnp.dot(act, wd_ref[...], preferred_element_type=F32)

    @pl.when(j == pl.num_programs(1) - 1)
    def _():
        o_ref[...] = x_ref[...] + acc_ref[...] * keep_ref[...]


def _ffn(x, g, w_up, conv_w, conv_b, w_down, keep):
    m, d = x.shape
    f = w_down.shape[0]
    nf = f // FF_TILE
    tm = ROW_TILE
    halo_blocks = tm // HALO
    return pl.pallas_call(
        _ffn_kernel,
        grid=(m // tm, nf),
        in_specs=[
            pl.BlockSpec((tm, d), lambda i, j: (i, 0)),
            pl.BlockSpec((None, HALO, d), lambda i, j: (jnp.maximum(i * halo_blocks - 1, 0), 0, 0)),
            pl.BlockSpec((1, d), lambda i, j: (0, 0)),
            pl.BlockSpec((d, FF_TILE), lambda i, j: (0, j)),
            pl.BlockSpec((d, FF_TILE), lambda i, j: (0, nf + j)),
            pl.BlockSpec((CONV_W, FF_TILE), lambda i, j: (0, j)),
            pl.BlockSpec((CONV_W, FF_TILE), lambda i, j: (0, nf + j)),
            pl.BlockSpec((1, FF_TILE), lambda i, j: (0, j)),
            pl.BlockSpec((1, FF_TILE), lambda i, j: (0, nf + j)),
            pl.BlockSpec((FF_TILE, d), lambda i, j: (j, 0)),
            pl.BlockSpec((tm, 1), lambda i, j: (i, 0)),
        ],
        out_specs=pl.BlockSpec((tm, d), lambda i, j: (i, 0)),
        out_shape=jax.ShapeDtypeStruct((m, d), F32),
        scratch_shapes=[pltpu.VMEM((tm + HALO, d), BF16), pltpu.VMEM((tm, d), F32)],
        compiler_params=_params(("parallel", "arbitrary")),
        name="conv_ffn",
    )(x, x.reshape(m // HALO, HALO, d), g.reshape(1, d), w_up, w_up, conv_w, conv_w,
      conv_b.reshape(1, -1), conv_b.reshape(1, -1), w_down, keep)


def _rope_tables(pos, inv_freq):
    ang = pos.astype(F32)[:, None] * inv_freq[None, :]
    return jnp.cos(ang), jnp.sin(ang)


def kernel(x, meta, mix_norm_e, w_in_e, gla_w_gate_e, gla_b_gate_e, gla_norm_e, ret_norm_e, w_out_e, mix_norm_o, w_in_o, q_norm_o, k_norm_o, lam_q1_o, lam_k1_o, lam_q2_o, lam_k2_o, diff_norm_o, w_out_o, ffn_norm, w_up, conv_w, conv_b, w_down):
    nb, s, d = x.shape
    seq_valid_end = CHUNK + s
    lpad = -(-seq_valid_end // ATTN_TILE) * ATTN_TILE
    m = nb * lpad

    h = jnp.concatenate([jnp.zeros((nb, META_OFFSET, d), x.dtype),
                         jnp.broadcast_to(meta.astype(x.dtype)[None], (nb, N_META, d)),
                         x,
                         jnp.zeros((nb, lpad - seq_valid_end, d), x.dtype)], axis=1).reshape(m, d)
    idx = jnp.arange(lpad)
    valid = (idx >= META_OFFSET) & (idx < seq_valid_end)
    pos = idx - META_OFFSET
    keep = jnp.tile(valid.astype(F32), nb).reshape(m, 1)

    gla_cols = 2 * GLA_HEADS * GLA_DK + 2 * GLA_HEADS * GLA_DV
    w_in = w_in_e[0]
    w_main = jnp.concatenate([w_in[:, :gla_cols], w_in[:, gla_cols + GLA_GATE_RANK:]], axis=1).astype(BF16)
    w_lr = jnp.pad(w_in[:, gla_cols:gla_cols + GLA_GATE_RANK], ((0, 0), (0, LANES - GLA_GATE_RANK))).astype(BF16)
    w_gate2 = jnp.pad(gla_w_gate_e[0], ((0, LANES - GLA_GATE_RANK), (0, 0))).astype(BF16)
    proj = _norm_matmul(h, mix_norm_e[0], w_main, 1024).reshape(nb, lpad, -1)
    lr = _norm_matmul(h, mix_norm_e[0], w_lr, LANES).reshape(nb, lpad, LANES)
    sel = (jnp.arange(CHUNK)[:, None, None] == jnp.arange(CHUNK)[None, None, :]).astype(BF16)
    sel = jnp.broadcast_to(sel, (CHUNK, GLA_DK, CHUNK)).reshape(CHUNK // PAIR_BLOCK, PAIR_BLOCK * GLA_DK, CHUNK)
    o_a = _gla(proj, lr, w_gate2, gla_b_gate_e[0].reshape(1, -1), gla_norm_e[0].reshape(GLA_HEADS, 1, GLA_DV),
               sel, lpad, seq_valid_end)
    inv_freq_ret = 1.0 / (ROPE_THETA ** jnp.linspace(0.0, 1.0, RET_DK // 2, dtype=F32))
    cos_r, sin_r = _rope_tables(pos, inv_freq_ret)
    log_gamma = jnp.log(1.0 - 2.0 ** (-5.0 - jnp.arange(RET_HEADS, dtype=F32))).reshape(RET_HEADS, 1, 1)
    o_b = _ret(proj, cos_r, sin_r, log_gamma, ret_norm_e[0].reshape(RET_HEADS, 1, RET_DV), lpad, seq_valid_end)
    h = _matmul_res([o_a.reshape(m, -1), o_b.reshape(m, -1)], w_out_e[0].astype(BF16), h, keep)
    h = _ffn(h, ffn_norm[0], w_up[0].astype(BF16), conv_w[0], conv_b[0], w_down[0].astype(BF16), keep)

    layer = 1
    lam_init = 0.8 - 0.6 * math.exp(-0.3 * layer)
    qk_cols = 4 * DIFF_HEADS * DIFF_DK
    inv_freq = 1.0 / (ROPE_THETA ** (jnp.arange(0, DIFF_DK, 2, dtype=F32) / DIFF_DK))
    cos_d, sin_d = _rope_tables(pos, inv_freq)
    cos_full = jnp.tile(jnp.concatenate([cos_d, cos_d], axis=-1), (nb, 1))
    sin_signed = jnp.tile(jnp.concatenate([-sin_d, sin_d], axis=-1), (nb, 1))
    gains = jnp.stack([q_norm_o[0] * (DIFF_DK ** -0.5 * LOG2E), k_norm_o[0]]).reshape(2, 1, DIFF_DK)
    qk = _norm_matmul_qk(h, mix_norm_o[0], w_in_o[0][:, :qk_cols].astype(BF16), cos_full, sin_signed, gains,
                         1024).reshape(nb, lpad, qk_cols)
    vt = _norm_matmul_t(h, mix_norm_o[0], w_in_o[0][:, qk_cols:].T.astype(BF16), 1024)
    lam_vecs = jnp.stack([lam_q1_o[0], lam_k1_o[0], lam_q2_o[0], lam_k2_o[0]]).astype(F32)
    o = _diff_attn(qk, vt, lam_vecs, diff_norm_o[0].reshape(1, -1), lpad, seq_valid_end, lam_init)
    h = _matmul_res([o.reshape(m, -1)], w_out_o[0].astype(BF16), h, keep)
    h = _ffn(h, ffn_norm[1], w_up[1].astype(BF16), conv_w[1], conv_b[1], w_down[1].astype(BF16), keep)

    return h.reshape(nb, lpad, d)[:, CHUNK:CHUNK + s]
```

```python
import functools
import math

import jax
import jax.numpy as jnp
from jax import lax
from jax.experimental import pallas as pl
from jax.experimental.pallas import tpu as pltpu

F32 = jnp.float32
BF16 = jnp.bfloat16

D_MODEL = 2048
CHUNK = 64
N_META = 16
META_OFFSET = CHUNK - N_META
EPS = 1e-6
ROPE_THETA = 10000.0

GLA_HEADS = 4
GLA_DK = 128
GLA_DV = 256
GLA_GATE_RANK = 16
GLA_GATE_NORM = 16.0

RET_HEADS = 4
RET_DK = 256
RET_DV = 256

DIFF_HEADS = 8
DIFF_DK = 128
DIFF_DV = 256

D_FF = 5632
CONV_W = 3

LANES = 128
ATTN_TILE = 256
ATTN_Q_TILE = 512
ROW_TILE = 512
PROJ_TILE = 512
HALO = 16
FF_TILE = 512
VMEM_LIMIT = 56 * 2**20
PAIR_BLOCK = 8
LOG2E = math.log2(math.e)


def _params(semantics):
    return pltpu.CompilerParams(dimension_semantics=semantics, vmem_limit_bytes=VMEM_LIMIT)


def _rms(x, g):
    return x * lax.rsqrt(jnp.mean(x * x, axis=-1, keepdims=True) + EPS) * g


def _sigmoid(x):
    return 1.0 / (1.0 + jnp.exp(-x))


def _norm_matmul_kernel(x_ref, g_ref, w_ref, o_ref, xn_ref):
    @pl.when(pl.program_id(1) == 0)
    def _():
        xn_ref[...] = _rms(x_ref[...], g_ref[...]).astype(BF16)

    o_ref[...] = jnp.dot(xn_ref[...], w_ref[...], preferred_element_type=F32).astype(o_ref.dtype)


def _norm_matmul(x, g, w, tn):
    m, d = x.shape
    n = w.shape[1]
    return pl.pallas_call(
        _norm_matmul_kernel,
        grid=(m // PROJ_TILE, n // tn),
        in_specs=[
            pl.BlockSpec((PROJ_TILE, d), lambda i, j: (i, 0)),
            pl.BlockSpec((1, d), lambda i, j: (0, 0)),
            pl.BlockSpec((d, tn), lambda i, j: (0, j)),
        ],
        out_specs=pl.BlockSpec((PROJ_TILE, tn), lambda i, j: (i, j)),
        out_shape=jax.ShapeDtypeStruct((m, n), BF16),
        scratch_shapes=[pltpu.VMEM((PROJ_TILE, d), BF16)],
        compiler_params=_params(("parallel", "arbitrary")),
        name="norm_matmul",
    )(x, g.reshape(1, d), w)


def _norm_matmul_qk_kernel(x_ref, g_ref, w_ref, cos_ref, sin_ref, gain_ref, o_ref, xn_ref):
    @pl.when(pl.program_id(1) == 0)
    def _():
        xn_ref[...] = _rms(x_ref[...], g_ref[...]).astype(BF16)

    y = jnp.dot(xn_ref[...], w_ref[...], preferred_element_type=F32)
    cos = cos_ref[...]
    sin = sin_ref[...]
    gain = gain_ref[...]
    for s in range(y.shape[1] // DIFF_DK):
        sl = slice(s * DIFF_DK, (s + 1) * DIFF_DK)
        ys = _rms(y[:, sl], gain)
        o_ref[:, sl] = (ys * cos + pltpu.roll(ys, DIFF_DK // 2, axis=1) * sin).astype(o_ref.dtype)


def _norm_matmul_qk(x, g, w, cos, sin, gains, tn):
    m, d = x.shape
    n = w.shape[1]
    tiles_per_gain = n // (2 * tn)
    return pl.pallas_call(
        _norm_matmul_qk_kernel,
        grid=(m // PROJ_TILE, n // tn),
        in_specs=[
            pl.BlockSpec((PROJ_TILE, d), lambda i, j: (i, 0)),
            pl.BlockSpec((1, d), lambda i, j: (0, 0)),
            pl.BlockSpec((d, tn), lambda i, j: (0, j)),
            pl.BlockSpec((PROJ_TILE, DIFF_DK), lambda i, j: (i, 0)),
            pl.BlockSpec((PROJ_TILE, DIFF_DK), lambda i, j: (i, 0)),
            pl.BlockSpec((None, 1, DIFF_DK), lambda i, j: (j // tiles_per_gain, 0, 0)),
        ],
        out_specs=pl.BlockSpec((PROJ_TILE, tn), lambda i, j: (i, j)),
        out_shape=jax.ShapeDtypeStruct((m, n), BF16),
        scratch_shapes=[pltpu.VMEM((PROJ_TILE, d), BF16)],
        compiler_params=_params(("parallel", "arbitrary")),
        name="norm_matmul_qk",
    )(x, g.reshape(1, d), w, cos, sin, gains)


def _norm_matmul_t_kernel(x_ref, g_ref, wt_ref, o_ref, xn_ref):
    @pl.when(pl.program_id(1) == 0)
    def _():
        xn_ref[...] = _rms(x_ref[...], g_ref[...]).astype(BF16)

    o_ref[...] = lax.dot_general(wt_ref[...], xn_ref[...], (((1,), (1,)), ((), ())),
                                 preferred_element_type=F32).astype(o_ref.dtype)


def _norm_matmul_t(x, g, wt, tn):
    m, d = x.shape
    n = wt.shape[0]
    return pl.pallas_call(
        _norm_matmul_t_kernel,
        grid=(m // PROJ_TILE, n // tn),
        in_specs=[
            pl.BlockSpec((PROJ_TILE, d), lambda i, j: (i, 0)),
            pl.BlockSpec((1, d), lambda i, j: (0, 0)),
            pl.BlockSpec((tn, d), lambda i, j: (j, 0)),
        ],
        out_specs=pl.BlockSpec((tn, PROJ_TILE), lambda i, j: (j, i)),
        out_shape=jax.ShapeDtypeStruct((n, m), BF16),
        scratch_shapes=[pltpu.VMEM((PROJ_TILE, d), BF16)],
        compiler_params=_params(("parallel", "arbitrary")),
        name="norm_matmul_t",
    )(x, g.reshape(1, d), wt)


def _matmul_res_kernel(n_in, *refs):
    a_refs = refs[:n_in]
    w_ref, h_ref, keep_ref, o_ref = refs[n_in:]
    mix = None
    k0 = 0
    for a_ref in a_refs:
        k1 = k0 + a_ref.shape[1]
        part = jnp.dot(a_ref[...], w_ref[k0:k1, :], preferred_element_type=F32)
        mix = part if mix is None else mix + part
        k0 = k1
    o_ref[...] = h_ref[...] + mix * keep_ref[...]


def _matmul_res(a_parts, w, h, keep):
    m = h.shape[0]
    k, n = w.shape
    return pl.pallas_call(
        functools.partial(_matmul_res_kernel, len(a_parts)),
        grid=(m // ROW_TILE,),
        in_specs=[pl.BlockSpec((ROW_TILE, a.shape[1]), lambda i: (i, 0)) for a in a_parts] + [
            pl.BlockSpec((k, n), lambda i: (0, 0)),
            pl.BlockSpec((ROW_TILE, n), lambda i: (i, 0)),
            pl.BlockSpec((ROW_TILE, 1), lambda i: (i, 0)),
        ],
        out_specs=pl.BlockSpec((ROW_TILE, n), lambda i: (i, 0)),
        out_shape=jax.ShapeDtypeStruct((m, n), F32),
        compiler_params=_params(("parallel",)),
        name="matmul_res",
    )(*a_parts, w, h, keep)


def _gla_kernel(seq_valid_end, q_ref, k_ref, v_ref, g_ref, lr_ref, w2_ref, b_ref, nrm_ref, sel_ref,
                o_ref, st_ref, cum_ref, qs_ref, ks_ref):
    c = pl.program_id(1)

    @pl.when(c == 0)
    def _():
        st_ref[...] = jnp.zeros_like(st_ref)

    row = c * CHUNK + lax.broadcasted_iota(jnp.int32, (CHUNK, 1), 0)
    vm = jnp.where(row >= META_OFFSET, jnp.where(row < seq_valid_end, 1.0, 0.0), 0.0)
    ii = lax.broadcasted_iota(jnp.int32, (CHUNK, CHUNK), 0)
    jj = lax.broadcasted_iota(jnp.int32, (CHUNK, CHUNK), 1)
    tril = jnp.where(jj <= ii, 1.0, 0.0).astype(F32)

    gate = jnp.dot(lr_ref[...], w2_ref[...], preferred_element_type=F32) + b_ref[...]
    log_a = (jnp.minimum(gate, 0.0) - jnp.log(1.0 + jnp.exp(-jnp.abs(gate)))) * (1.0 / GLA_GATE_NORM) * vm

    for h in range(GLA_HEADS):
        ksl = slice(h * GLA_DK, (h + 1) * GLA_DK)
        cum_ref[h] = jnp.dot(tril, log_a[:, ksl], preferred_element_type=F32,
                             precision=lax.Precision.HIGHEST) * LOG2E
        qs_ref[h] = q_ref[:, ksl].astype(F32) * (GLA_DK ** -0.5)
        ks_ref[h] = k_ref[:, ksl].astype(F32) * vm

    def decay(cum, cj, r0):
        r1 = r0 + PAIR_BLOCK
        parts = [jnp.exp2(-jnp.abs(cum[r0:r1] - cj))]
        if r0 > 0:
            parts.insert(0, jnp.exp2(cj - cum[:r0]))
        if r1 < CHUNK:
            parts.append(jnp.exp2(cum[r1:] - cj))
        return jnp.concatenate(parts, axis=0)

    scores = []
    for h in range(GLA_HEADS):
        cum = cum_ref[h]
        q = qs_ref[h]
        total = None
        for jb in range(CHUNK // PAIR_BLOCK):
            ws = []
            for j in range(jb * PAIR_BLOCK, (jb + 1) * PAIR_BLOCK):
                cj = cum_ref[h, j:j + 1, :]
                kj = ks_ref[h, j:j + 1, :]
                ws.append((decay(cum, cj, jb * PAIR_BLOCK) * (q * kj)).astype(BF16))
            part = jnp.dot(jnp.concatenate(ws, axis=1), sel_ref[jb], preferred_element_type=F32)
            total = part if total is None else total + part
        scores.append(total)

    for h in range(GLA_HEADS):
        vsl = slice(h * GLA_DV, (h + 1) * GLA_DV)
        cum = cum_ref[h]
        v = v_ref[:, vsl]
        st = st_ref[h]
        q_in = (qs_ref[h] * jnp.exp2(cum)).astype(BF16)
        o = jnp.dot(scores[h].astype(BF16), v, preferred_element_type=F32)
        o = o + lax.dot_general(q_in, st.astype(BF16), (((1,), (1,)), ((), ())), preferred_element_type=F32)
        last = cum[CHUNK - 1:CHUNK, :]
        k_dec = (ks_ref[h] * jnp.exp2(last - cum)).astype(BF16)
        st_ref[h] = st * jnp.exp2(last) + lax.dot_general(v, k_dec, (((0,), (0,)), ((), ())),
                                                          preferred_element_type=F32)
        gv = g_ref[:, vsl].astype(F32)
        o_ref[:, vsl] = (_rms(o, nrm_ref[h]) * (gv * _sigmoid(gv))).astype(o_ref.dtype)


def _gla(proj, lr, w2, b_gate, nrm, sel, lpad, seq_valid_end):
    nb = proj.shape[0]
    hk = GLA_HEADS * GLA_DK
    hv = GLA_HEADS * GLA_DV
    return pl.pallas_call(
        functools.partial(_gla_kernel, seq_valid_end),
        grid=(nb, lpad // CHUNK),
        in_specs=[
            pl.BlockSpec((None, CHUNK, hk), lambda b, c: (b, c, 0)),
            pl.BlockSpec((None, CHUNK, hk), lambda b, c: (b, c, 1)),
            pl.BlockSpec((None, CHUNK, hv), lambda b, c: (b, c, 1)),
            pl.BlockSpec((None, CHUNK, hv), lambda b, c: (b, c, 2)),
            pl.BlockSpec((None, CHUNK, LANES), lambda b, c: (b, c, 0)),
            pl.BlockSpec((LANES, hk), lambda b, c: (0, 0)),
            pl.BlockSpec((1, hk), lambda b, c: (0, 0)),
            pl.BlockSpec((GLA_HEADS, 1, GLA_DV), lambda b, c: (0, 0, 0)),
            pl.BlockSpec((CHUNK // PAIR_BLOCK, PAIR_BLOCK * GLA_DK, CHUNK), lambda b, c: (0, 0, 0)),
        ],
        out_specs=pl.BlockSpec((None, CHUNK, hv), lambda b, c: (b, c, 0)),
        out_shape=jax.ShapeDtypeStruct((nb, lpad, hv), BF16),
        scratch_shapes=[
            pltpu.VMEM((GLA_HEADS, GLA_DV, GLA_DK), F32),
            pltpu.VMEM((GLA_HEADS, CHUNK, GLA_DK), F32),
            pltpu.VMEM((GLA_HEADS, CHUNK, GLA_DK), F32),
            pltpu.VMEM((GLA_HEADS, CHUNK, GLA_DK), F32),
        ],
        compiler_params=_params(("parallel", "arbitrary")),
        name="gla_scan",
    )(proj, proj, proj, proj, lr, w2, b_gate, nrm, sel)


def _ret_kernel(seq_valid_end, q_ref, k_ref, v_ref, g_ref, cos_ref, sin_ref, lg_ref, nrm_ref,
                o_ref, st_ref):
    c = pl.program_id(1)

    @pl.when(c == 0)
    def _():
        st_ref[...] = jnp.zeros_like(st_ref)

    row = c * CHUNK + lax.broadcasted_iota(jnp.int32, (CHUNK, 1), 0)
    vm = jnp.where(row >= META_OFFSET, jnp.where(row < seq_valid_end, 1.0, 0.0), 0.0)
    n_i = lax.broadcasted_iota(jnp.int32, (CHUNK, CHUNK), 0).astype(F32)
    n_j = lax.broadcasted_iota(jnp.int32, (CHUNK, CHUNK), 1).astype(F32)
    dist = jnp.abs(n_i - n_j)
    n_col = lax.broadcasted_iota(jnp.int32, (CHUNK, 1), 0).astype(F32)
    cos = cos_ref[...]
    sin = sin_ref[...]
    half = RET_DK // 2

    def rot(x):
        x1 = x[:, :half]
        x2 = x[:, half:]
        return jnp.concatenate([x1 * cos - x2 * sin, x2 * cos + x1 * sin], axis=-1)

    for h in range(RET_HEADS):
        ksl = slice(h * RET_DK, (h + 1) * RET_DK)
        vsl = slice(h * RET_DV, (h + 1) * RET_DV)
        lg = lg_ref[h]
        intra = jnp.exp(dist * lg)
        xi = jnp.exp((n_col + 1.0) * lg)
        zeta = jnp.exp((CHUNK - 1.0 - n_col) * lg)
        g_chunk = jnp.exp(CHUNK * lg)
        qr = rot(q_ref[:, ksl].astype(F32)).astype(BF16)
        kr = rot(k_ref[:, ksl].astype(F32)) * (RET_DK ** -0.5) * vm
        v = v_ref[:, vsl]
        st = st_ref[h]
        s = lax.dot_general(qr, kr.astype(BF16), (((1,), (1,)), ((), ())), preferred_element_type=F32) * intra
        o = jnp.dot(s.astype(BF16), v, preferred_element_type=F32)
        o = o + lax.dot_general(qr, st.astype(BF16), (((1,), (1,)), ((), ())), preferred_element_type=F32) * xi
        st_ref[h] = st * g_chunk + lax.dot_general(v, (kr * zeta).astype(BF16), (((0,), (0,)), ((), ())),
                                                   preferred_element_type=F32)
        gv = g_ref[:, vsl].astype(F32)
        o_ref[:, vsl] = (_rms(o, nrm_ref[h]) * (gv * _sigmoid(gv))).astype(o_ref.dtype)


def _ret(proj, cos, sin, lg, nrm, lpad, seq_valid_end):
    nb = proj.shape[0]
    hk = RET_HEADS * RET_DK
    hv = RET_HEADS * RET_DV
    base = (2 * GLA_HEADS * GLA_DK + 2 * GLA_HEADS * GLA_DV) // hk
    return pl.pallas_call(
        functools.partial(_ret_kernel, seq_valid_end),
        grid=(nb, lpad // CHUNK),
        in_specs=[
            pl.BlockSpec((None, CHUNK, hk), lambda b, c: (b, c, base)),
            pl.BlockSpec((None, CHUNK, hk), lambda b, c: (b, c, base + 1)),
            pl.BlockSpec((None, CHUNK, hv), lambda b, c: (b, c, base + 2)),
            pl.BlockSpec((None, CHUNK, hv), lambda b, c: (b, c, base + 3)),
            pl.BlockSpec((CHUNK, RET_DK // 2), lambda b, c: (c, 0)),
            pl.BlockSpec((CHUNK, RET_DK // 2), lambda b, c: (c, 0)),
            pl.BlockSpec((RET_HEADS, 1, 1), lambda b, c: (0, 0, 0)),
            pl.BlockSpec((RET_HEADS, 1, RET_DV), lambda b, c: (0, 0, 0)),
        ],
        out_specs=pl.BlockSpec((None, CHUNK, hv), lambda b, c: (b, c, 0)),
        out_shape=jax.ShapeDtypeStruct((nb, lpad, hv), BF16),
        scratch_shapes=[pltpu.VMEM((RET_HEADS, RET_DV, RET_DK), F32)],
        compiler_params=_params(("parallel", "arbitrary")),
        name="ret_scan",
    )(proj, proj, proj, proj, cos, sin, lg, nrm)


def _diff_attn_kernel(seq_valid_end, lam_init, lpad, q1_ref, q2_ref, k1_ref, k2_ref, vt_ref, lam_ref, nrm_ref,
                      o_ref, acc1_ref, acc2_ref):
    qi = pl.program_id(2)
    t = ATTN_TILE
    tq = ATTN_Q_TILE
    qs = (q1_ref[...], q2_ref[...])
    k_refs = (k1_ref, k2_ref)
    acc_refs = (acc1_ref, acc2_ref)
    q_chunk = lax.shift_right_logical(qi * tq + lax.broadcasted_iota(jnp.int32, (1, tq), 1), 6)
    nt_dims = (((1,), (1,)), ((), ()))

    def start(j):
        return j * t if isinstance(j, int) else pl.multiple_of(j * t, t)

    def scores(j, masked):
        out = []
        if masked:
            k_idx = j * t + lax.broadcasted_iota(jnp.int32, (t, 1), 0)
            k_chunk = jnp.where(k_idx >= META_OFFSET,
                                jnp.where(k_idx < seq_valid_end, lax.shift_right_logical(k_idx, 6), 2**30), 2**30)
            mask = k_chunk <= q_chunk
        for b in range(2):
            s = lax.dot_general(k_refs[b][pl.ds(start(j), t), :], qs[b], nt_dims, preferred_element_type=F32)
            out.append(jnp.where(mask, s, -1e30) if masked else s)
        return tuple(out)

    def update(j, stats, s_cur):
        vt = vt_ref[:, pl.ds(start(j), t)]
        new = []
        for b in range(2):
            m, l = stats[b]
            s = s_cur[b]
            m_new = jnp.maximum(m, jnp.max(s, axis=0, keepdims=True))
            alpha = jnp.exp2(m - m_new)
            p = jnp.exp2(s - m_new)
            l_new = alpha * l + jnp.sum(p, axis=0, keepdims=True)
            acc_refs[b][...] = alpha * acc_refs[b][...] + jnp.dot(vt, p.astype(BF16), preferred_element_type=F32)
            new.append((m_new, l_new))
        return tuple(new)

    def masked_body(j, carry):
        stats, s_cur = carry
        s_next = scores(j + 1, True)
        return update(j, stats, s_cur), s_next

    def pair_body(p, carry):
        stats, s_cur = carry
        j = 2 * p
        s_mid = scores(j + 1, False)
        stats = update(j, stats, s_cur)
        s_next = scores(j + 2, False)
        return update(j + 1, stats, s_mid), s_next

    for b in range(2):
        acc_refs[b][...] = jnp.zeros_like(acc_refs[b])
    init = (jnp.full((1, tq), -1e30, F32), jnp.zeros((1, tq), F32))
    carry = ((init, init), scores(0, True))
    first_diag = qi * (tq // t)
    n_tiles = jnp.minimum(first_diag + tq // t, lpad // t)
    n_plain = jnp.maximum(first_diag - 1, 0)
    n_pairs = lax.shift_right_logical(n_plain, 1)
    carry = lax.fori_loop(0, n_pairs, pair_body, carry)
    carry = lax.fori_loop(2 * n_pairs, n_tiles - 1, masked_body, carry)
    (_, l1), (_, l2) = update(n_tiles - 1, *carry)

    lam_v = lam_ref[...]
    lam = (jnp.exp(jnp.sum(lam_v[0:1] * lam_v[1:2], axis=-1, keepdims=True))
           - jnp.exp(jnp.sum(lam_v[2:3] * lam_v[3:4], axis=-1, keepdims=True)) + lam_init)
    o_t = acc1_ref[...] * (1.0 / l1) - acc2_ref[...] * (lam * (1.0 / l2))
    o_ref[...] = (_rms(o_t.T, nrm_ref[...]) * (1.0 - lam_init)).astype(o_ref.dtype)


def _diff_attn(qk, vt, lam_vecs, nrm, lpad, seq_valid_end, lam_init):
    nb = qk.shape[0]
    t = ATTN_TILE
    tq = ATTN_Q_TILE
    kb = 2 * DIFF_HEADS
    return pl.pallas_call(
        functools.partial(_diff_attn_kernel, seq_valid_end, lam_init, lpad),
        grid=(nb, DIFF_HEADS, pl.cdiv(lpad, tq)),
        in_specs=[
            pl.BlockSpec((None, tq, DIFF_DK), lambda b, h, i: (b, i, 2 * h)),
            pl.BlockSpec((None, tq, DIFF_DK), lambda b, h, i: (b, i, 2 * h + 1)),
            pl.BlockSpec((None, lpad, DIFF_DK), lambda b, h, i: (b, 0, kb + 2 * h)),
            pl.BlockSpec((None, lpad, DIFF_DK), lambda b, h, i: (b, 0, kb + 2 * h + 1)),
            pl.BlockSpec((DIFF_DV, lpad), lambda b, h, i: (h, b)),
            pl.BlockSpec((4, DIFF_DK), lambda b, h, i: (0, 0)),
            pl.BlockSpec((1, DIFF_DV), lambda b, h, i: (0, 0)),
        ],
        out_specs=pl.BlockSpec((None, tq, DIFF_DV), lambda b, h, i: (b, i, h)),
        out_shape=jax.ShapeDtypeStruct((nb, lpad, DIFF_HEADS * DIFF_DV), BF16),
        scratch_shapes=[pltpu.VMEM((DIFF_DV, tq), F32), pltpu.VMEM((DIFF_DV, tq), F32)],
        compiler_params=_params(("parallel", "parallel", "arbitrary")),
        name="diff_attn",
    )(qk, qk, qk, qk, vt, lam_vecs, nrm)


def _ffn_kernel(x_ref, halo_ref, g_ref, wv_ref, wg_ref, cwv_ref, cwg_ref, cbv_ref, cbg_ref, wd_ref, keep_ref,
                o_ref, hn_ref, acc_ref):
    i = pl.program_id(0)
    j = pl.program_id(1)
    tm = x_ref.shape[0]

    @pl.when(j == 0)
    def _():
        g = g_ref[...]
        hn_ref[HALO:, :] = _rms(x_ref[...], g).astype(BF16)
        first = jnp.where(i > 0, 1.0, 0.0)
        hn_ref[:HALO, :] = (_rms(halo_ref[...], g) * first).astype(BF16)
        acc_ref[...] = jnp.zeros_like(acc_ref)

    hn = hn_ref[...]

    def conv_branch(w_ref, cw_ref, cb_ref):
        u = jnp.dot(hn, w_ref[...], preferred_element_type=F32)
        cw = cw_ref[...]
        c = cb_ref[...] + cw[0:1] * u[HALO - 2:HALO - 2 + tm]
        c = c + cw[1:2] * u[HALO - 1:HALO - 1 + tm]
        return c + cw[2:3] * u[HALO:HALO + tm]

    val = conv_branch(wv_ref, cwv_ref, cbv_ref)
    gate = conv_branch(wg_ref, cwg_ref, cbg_ref)
    act = (val * (gate * _sigmoid(gate))).astype(BF16)
    acc_ref[...] += jnp.dot(act, wd_ref[...], preferred_element_type=F32)

    @pl.when(j == pl.num_programs(1) - 1)
    def _():
        o_ref[...] = x_ref[...] + acc_ref[...] * keep_ref[...]


def _ffn(x, g, w_up, conv_w, conv_b, w_down, keep):
    m, d = x.shape
    f = w_down.shape[0]
    nf = f // FF_TILE
    tm = ROW_TILE
    halo_blocks = tm // HALO
    return pl.pallas_call(
        _ffn_kernel,
        grid=(m // tm, nf),
        in_specs=[
            pl.BlockSpec((tm, d), lambda i, j: (i, 0)),
            pl.BlockSpec((None, HALO, d), lambda i, j: (jnp.maximum(i * halo_blocks - 1, 0), 0, 0)),
            pl.BlockSpec((1, d), lambda i, j: (0, 0)),
            pl.BlockSpec((d, FF_TILE), lambda i, j: (0, j)),
            pl.BlockSpec((d, FF_TILE), lambda i, j: (0, nf + j)),
            pl.BlockSpec((CONV_W, FF_TILE), lambda i, j: (0, j)),
            pl.BlockSpec((CONV_W, FF_TILE), lambda i, j: (0, nf + j)),
            pl.BlockSpec((1, FF_TILE), lambda i, j: (0, j)),
            pl.BlockSpec((1, FF_TILE), lambda i, j: (0, nf + j)),
            pl.BlockSpec((FF_TILE, d), lambda i, j: (j, 0)),
            pl.BlockSpec((tm, 1), lambda i, j: (i, 0)),
        ],
        out_specs=pl.BlockSpec((tm, d), lambda i, j: (i, 0)),
        out_shape=jax.ShapeDtypeStruct((m, d), F32),
        scratch_shapes=[pltpu.VMEM((tm + HALO, d), BF16), pltpu.VMEM((tm, d), F32)],
        compiler_params=_params(("parallel", "arbitrary")),
        name="conv_ffn",
    )(x, x.reshape(m // HALO, HALO, d), g.reshape(1, d), w_up, w_up, conv_w, conv_w,
      conv_b.reshape(1, -1), conv_b.reshape(1, -1), w_down, keep)


def _rope_tables(pos, inv_freq):
    ang = pos.astype(F32)[:, None] * inv_freq[None, :]
    return jnp.cos(ang), jnp.sin(ang)


def kernel(x, meta, mix_norm_e, w_in_e, gla_w_gate_e, gla_b_gate_e, gla_norm_e, ret_norm_e, w_out_e, mix_norm_o, w_in_o, q_norm_o, k_norm_o, lam_q1_o, lam_k1_o, lam_q2_o, lam_k2_o, diff_norm_o, w_out_o, ffn_norm, w_up, conv_w, conv_b, w_down):
    nb, s, d = x.shape
    seq_valid_end = CHUNK + s
    lpad = -(-seq_valid_end // ATTN_TILE) * ATTN_TILE
    m = nb * lpad

    h = jnp.concatenate([jnp.zeros((nb, META_OFFSET, d), x.dtype),
                         jnp.broadcast_to(meta.astype(x.dtype)[None], (nb, N_META, d)),
                         x,
                         jnp.zeros((nb, lpad - seq_valid_end, d), x.dtype)], axis=1).reshape(m, d)
    idx = jnp.arange(lpad)
    valid = (idx >= META_OFFSET) & (idx < seq_valid_end)
    pos = idx - META_OFFSET
    keep = jnp.tile(valid.astype(F32), nb).reshape(m, 1)

    gla_cols = 2 * GLA_HEADS * GLA_DK + 2 * GLA_HEADS * GLA_DV
    w_in = w_in_e[0]
    w_main = jnp.concatenate([w_in[:, :gla_cols], w_in[:, gla_cols + GLA_GATE_RANK:]], axis=1).astype(BF16)
    w_lr = jnp.pad(w_in[:, gla_cols:gla_cols + GLA_GATE_RANK], ((0, 0), (0, LANES - GLA_GATE_RANK))).astype(BF16)
    w_gate2 = jnp.pad(gla_w_gate_e[0], ((0, LANES - GLA_GATE_RANK), (0, 0))).astype(BF16)
    proj = _norm_matmul(h, mix_norm_e[0], w_main, 1024).reshape(nb, lpad, -1)
    lr = _norm_matmul(h, mix_norm_e[0], w_lr, LANES).reshape(nb, lpad, LANES)
    sel = (jnp.arange(CHUNK)[:, None, None] == jnp.arange(CHUNK)[None, None, :]).astype(BF16)
    sel = jnp.broadcast_to(sel, (CHUNK, GLA_DK, CHUNK)).reshape(CHUNK // PAIR_BLOCK, PAIR_BLOCK * GLA_DK, CHUNK)
    o_a = _gla(proj, lr, w_gate2, gla_b_gate_e[0].reshape(1, -1), gla_norm_e[0].reshape(GLA_HEADS, 1, GLA_DV),
               sel, lpad, seq_valid_end)
    inv_freq_ret = 1.0 / (ROPE_THETA ** jnp.linspace(0.0, 1.0, RET_DK // 2, dtype=F32))
    cos_r, sin_r = _rope_tables(pos, inv_freq_ret)
    log_gamma = jnp.log(1.0 - 2.0 ** (-5.0 - jnp.arange(RET_HEADS, dtype=F32))).reshape(RET_HEADS, 1, 1)
    o_b = _ret(proj, cos_r, sin_r, log_gamma, ret_norm_e[0].reshape(RET_HEADS, 1, RET_DV), lpad, seq_valid_end)
    h = _matmul_res([o_a.reshape(m, -1), o_b.reshape(m, -1)], w_out_e[0].astype(BF16), h, keep)
    h = _ffn(h, ffn_norm[0], w_up[0].astype(BF16), conv_w[0], conv_b[0], w_down[0].astype(BF16), keep)

    layer = 1
    lam_init = 0.8 - 0.6 * math.exp(-0.3 * layer)
    qk_cols = 4 * DIFF_HEADS * DIFF_DK
    inv_freq = 1.0 / (ROPE_THETA ** (jnp.arange(0, DIFF_DK, 2, dtype=F32) / DIFF_DK))
    cos_d, sin_d = _rope_tables(pos, inv_freq)
    cos_full = jnp.tile(jnp.concatenate([cos_d, cos_d], axis=-1), (nb, 1))
    sin_signed = jnp.tile(jnp.concatenate([-sin_d, sin_d], axis=-1), (nb, 1))
    gains = jnp.stack([q_norm_o[0] * (DIFF_DK ** -0.5 * LOG2E), k_norm_o[0]]).reshape(2, 1, DIFF_DK)
    qk = _norm_matmul_qk(h, mix_norm_o[0], w_in_o[0][:, :qk_cols].astype(BF16), cos_full, sin_signed, gains,
                         1024).reshape(nb, lpad, qk_cols)
    vt = _norm_matmul_t(h, mix_norm_o[0], w_in_o[0][:, qk_cols:].T.astype(BF16), 1024)
    lam_vecs = jnp.stack([lam_q1_o[0], lam_k1_o[0], lam_q2_o[0], lam_k2_o[0]]).astype(F32)
    o = _diff_attn(qk, vt, lam_vecs, diff_norm_o[0].reshape(1, -1), lpad, seq_valid_end, lam_init)
    h = _matmul_res([o.reshape(m, -1)], w_out_o[0].astype(BF16), h, keep)
    h = _ffn(h, ffn_norm[1], w_up[1].astype(BF16), conv_w[1], conv_b[1], w_down[1].astype(BF16), keep)

    return h.reshape(nb, lpad, d)[:, CHUNK:CHUNK + s]
```

```python
import functools
import math

import jax
import jax.numpy as jnp
from jax import lax
from jax.experimental import pallas as pl
from jax.experimental.pallas import tpu as pltpu

F32 = jnp.float32
BF16 = jnp.bfloat16

D_MODEL = 2048
CHUNK = 64
N_META = 16
META_OFFSET = CHUNK - N_META
EPS = 1e-6
ROPE_THETA = 10000.0

GLA_HEADS = 4
GLA_DK = 128
GLA_DV = 256
GLA_GATE_RANK = 16
GLA_GATE_NORM = 16.0

RET_HEADS = 4
RET_DK = 256
RET_DV = 256

DIFF_HEADS = 8
DIFF_DK = 128
DIFF_DV = 256

D_FF = 5632
CONV_W = 3

LANES = 128
ATTN_TILE = 256
ATTN_Q_TILE = 512
ROW_TILE = 512
PROJ_TILE = 512
HALO = 16
FF_TILE = 512
FF_SUB = 256
VMEM_LIMIT = 56 * 2**20
PAIR_BLOCK = 8
LOG2E = math.log2(math.e)


def _params(semantics):
    return pltpu.CompilerParams(dimension_semantics=semantics, vmem_limit_bytes=VMEM_LIMIT)


def _rms(x, g):
    return x * lax.rsqrt(jnp.mean(x * x, axis=-1, keepdims=True) + EPS) * g


def _sigmoid(x):
    return 1.0 / (1.0 + jnp.exp(-x))


def _norm_matmul_kernel(x_ref, g_ref, w_ref, o_ref, xn_ref):
    @pl.when(pl.program_id(1) == 0)
    def _():
        xn_ref[...] = _rms(x_ref[...], g_ref[...]).astype(BF16)

    o_ref[...] = jnp.dot(xn_ref[...], w_ref[...], preferred_element_type=F32).astype(o_ref.dtype)


def _norm_matmul(x, g, w, tn):
    m, d = x.shape
    n = w.shape[1]
    return pl.pallas_call(
        _norm_matmul_kernel,
        grid=(m // PROJ_TILE, n // tn),
        in_specs=[
            pl.BlockSpec((PROJ_TILE, d), lambda i, j: (i, 0)),
            pl.BlockSpec((1, d), lambda i, j: (0, 0)),
            pl.BlockSpec((d, tn), lambda i, j: (0, j)),
        ],
        out_specs=pl.BlockSpec((PROJ_TILE, tn), lambda i, j: (i, j)),
        out_shape=jax.ShapeDtypeStruct((m, n), BF16),
        scratch_shapes=[pltpu.VMEM((PROJ_TILE, d), BF16)],
        compiler_params=_params(("parallel", "arbitrary")),
        name="norm_matmul",
    )(x, g.reshape(1, d), w)


def _norm_matmul_qk_kernel(x_ref, g_ref, w_ref, cos_ref, sin_ref, gain_ref, o_ref, xn_ref):
    @pl.when(pl.program_id(1) == 0)
    def _():
        xn_ref[...] = _rms(x_ref[...], g_ref[...]).astype(BF16)

    y = jnp.dot(xn_ref[...], w_ref[...], preferred_element_type=F32)
    cos = cos_ref[...]
    sin = sin_ref[...]
    gain = gain_ref[...]
    for s in range(y.shape[1] // DIFF_DK):
        sl = slice(s * DIFF_DK, (s + 1) * DIFF_DK)
        ys = _rms(y[:, sl], gain)
        o_ref[:, sl] = (ys * cos + pltpu.roll(ys, DIFF_DK // 2, axis=1) * sin).astype(o_ref.dtype)


def _norm_matmul_qk(x, g, w, cos, sin, gains, tn):
    m, d = x.shape
    n = w.shape[1]
    tiles_per_gain = n // (2 * tn)
    return pl.pallas_call(
        _norm_matmul_qk_kernel,
        grid=(m // PROJ_TILE, n // tn),
        in_specs=[
            pl.BlockSpec((PROJ_TILE, d), lambda i, j: (i, 0)),
            pl.BlockSpec((1, d), lambda i, j: (0, 0)),
            pl.BlockSpec((d, tn), lambda i, j: (0, j)),
            pl.BlockSpec((PROJ_TILE, DIFF_DK), lambda i, j: (i, 0)),
            pl.BlockSpec((PROJ_TILE, DIFF_DK), lambda i, j: (i, 0)),
            pl.BlockSpec((None, 1, DIFF_DK), lambda i, j: (j // tiles_per_gain, 0, 0)),
        ],
        out_specs=pl.BlockSpec((PROJ_TILE, tn), lambda i, j: (i, j)),
        out_shape=jax.ShapeDtypeStruct((m, n), BF16),
        scratch_shapes=[pltpu.VMEM((PROJ_TILE, d), BF16)],
        compiler_params=_params(("parallel", "arbitrary")),
        name="norm_matmul_qk",
    )(x, g.reshape(1, d), w, cos, sin, gains)


def _norm_matmul_t_kernel(x_ref, g_ref, wt_ref, o_ref, xn_ref):
    @pl.when(pl.program_id(1) == 0)
    def _():
        xn_ref[...] = _rms(x_ref[...], g_ref[...]).astype(BF16)

    o_ref[...] = lax.dot_general(wt_ref[...], xn_ref[...], (((1,), (1,)), ((), ())),
                                 preferred_element_type=F32).astype(o_ref.dtype)


def _norm_matmul_t(x, g, wt, tn):
    m, d = x.shape
    n = wt.shape[0]
    return pl.pallas_call(
        _norm_matmul_t_kernel,
        grid=(m // PROJ_TILE, n // tn),
        in_specs=[
            pl.BlockSpec((PROJ_TILE, d), lambda i, j: (i, 0)),
            pl.BlockSpec((1, d), lambda i, j: (0, 0)),
            pl.BlockSpec((tn, d), lambda i, j: (j, 0)),
        ],
        out_specs=pl.BlockSpec((tn, PROJ_TILE), lambda i, j: (j, i)),
        out_shape=jax.ShapeDtypeStruct((n, m), BF16),
        scratch_shapes=[pltpu.VMEM((PROJ_TILE, d), BF16)],
        compiler_params=_params(("parallel", "arbitrary")),
        name="norm_matmul_t",
    )(x, g.reshape(1, d), wt)


def _matmul_res_kernel(n_in, *refs):
    a_refs = refs[:n_in]
    w_ref, h_ref, keep_ref, o_ref = refs[n_in:]
    mix = None
    k0 = 0
    for a_ref in a_refs:
        k1 = k0 + a_ref.shape[1]
        part = jnp.dot(a_ref[...], w_ref[k0:k1, :], preferred_element_type=F32)
        mix = part if mix is None else mix + part
        k0 = k1
    o_ref[...] = h_ref[...] + mix * keep_ref[...]


def _matmul_res(a_parts, w, h, keep):
    m = h.shape[0]
    k, n = w.shape
    return pl.pallas_call(
        functools.partial(_matmul_res_kernel, len(a_parts)),
        grid=(m // ROW_TILE,),
        in_specs=[pl.BlockSpec((ROW_TILE, a.shape[1]), lambda i: (i, 0)) for a in a_parts] + [
            pl.BlockSpec((k, n), lambda i: (0, 0)),
            pl.BlockSpec((ROW_TILE, n), lambda i: (i, 0)),
            pl.BlockSpec((ROW_TILE, 1), lambda i: (i, 0)),
        ],
        out_specs=pl.BlockSpec((ROW_TILE, n), lambda i: (i, 0)),
        out_shape=jax.ShapeDtypeStruct((m, n), F32),
        compiler_params=_params(("parallel",)),
        name="matmul_res",
    )(*a_parts, w, h, keep)


def _gla_kernel(seq_valid_end, q_ref, k_ref, v_ref, g_ref, lr_ref, w2_ref, b_ref, nrm_ref, sel_ref,
                o_ref, st_ref, cum_ref, qs_ref, ks_ref):
    c = pl.program_id(1)

    @pl.when(c == 0)
    def _():
        st_ref[...] = jnp.zeros_like(st_ref)

    row = c * CHUNK + lax.broadcasted_iota(jnp.int32, (CHUNK, 1), 0)
    vm = jnp.where(row >= META_OFFSET, jnp.where(row < seq_valid_end, 1.0, 0.0), 0.0)
    ii = lax.broadcasted_iota(jnp.int32, (CHUNK, CHUNK), 0)
    jj = lax.broadcasted_iota(jnp.int32, (CHUNK, CHUNK), 1)
    tril = jnp.where(jj <= ii, 1.0, 0.0).astype(F32)

    gate = jnp.dot(lr_ref[...], w2_ref[...], preferred_element_type=F32) + b_ref[...]
    log_a = (jnp.minimum(gate, 0.0) - jnp.log(1.0 + jnp.exp(-jnp.abs(gate)))) * (1.0 / GLA_GATE_NORM) * vm

    for h in range(GLA_HEADS):
        ksl = slice(h * GLA_DK, (h + 1) * GLA_DK)
        cum_ref[h] = jnp.dot(tril, log_a[:, ksl], preferred_element_type=F32,
                             precision=lax.Precision.HIGHEST) * LOG2E
        qs_ref[h] = q_ref[:, ksl].astype(F32) * (GLA_DK ** -0.5)
        ks_ref[h] = k_ref[:, ksl].astype(F32) * vm

    def decay(cum, cj, r0):
        r1 = r0 + PAIR_BLOCK
        parts = [jnp.exp2(-jnp.abs(cum[r0:r1] - cj))]
        if r0 > 0:
            parts.insert(0, jnp.exp2(cj - cum[:r0]))
        if r1 < CHUNK:
            parts.append(jnp.exp2(cum[r1:] - cj))
        return jnp.concatenate(parts, axis=0)

    scores = []
    for h in range(GLA_HEADS):
        cum = cum_ref[h]
        q = qs_ref[h]
        total = None
        for jb in range(CHUNK // PAIR_BLOCK):
            ws = []
            for j in range(jb * PAIR_BLOCK, (jb + 1) * PAIR_BLOCK):
                cj = cum_ref[h, j:j + 1, :]
                kj = ks_ref[h, j:j + 1, :]
                ws.append((decay(cum, cj, jb * PAIR_BLOCK) * (q * kj)).astype(BF16))
            part = jnp.dot(jnp.concatenate(ws, axis=1), sel_ref[jb], preferred_element_type=F32)
            total = part if total is None else total + part
        scores.append(total)

    for h in range(GLA_HEADS):
        vsl = slice(h * GLA_DV, (h + 1) * GLA_DV)
        cum = cum_ref[h]
        v = v_ref[:, vsl]
        st = st_ref[h]
        q_in = (qs_ref[h] * jnp.exp2(cum)).astype(BF16)
        o = jnp.dot(scores[h].astype(BF16), v, preferred_element_type=F32)
        o = o + lax.dot_general(q_in, st.astype(BF16), (((1,), (1,)), ((), ())), preferred_element_type=F32)
        last = cum[CHUNK - 1:CHUNK, :]
        k_dec = (ks_ref[h] * jnp.exp2(last - cum)).astype(BF16)
        st_ref[h] = st * jnp.exp2(last) + lax.dot_general(v, k_dec, (((0,), (0,)), ((), ())),
                                                          preferred_element_type=F32)
        gv = g_ref[:, vsl].astype(F32)
        o_ref[:, vsl] = (_rms(o, nrm_ref[h]) * (gv * _sigmoid(gv))).astype(o_ref.dtype)


def _gla(proj, lr, w2, b_gate, nrm, sel, lpad, seq_valid_end):
    nb = proj.shape[0]
    hk = GLA_HEADS * GLA_DK
    hv = GLA_HEADS * GLA_DV
    return pl.pallas_call(
        functools.partial(_gla_kernel, seq_valid_end),
        grid=(nb, lpad // CHUNK),
        in_specs=[
            pl.BlockSpec((None, CHUNK, hk), lambda b, c: (b, c, 0)),
            pl.BlockSpec((None, CHUNK, hk), lambda b, c: (b, c, 1)),
            pl.BlockSpec((None, CHUNK, hv), lambda b, c: (b, c, 1)),
            pl.BlockSpec((None, CHUNK, hv), lambda b, c: (b, c, 2)),
            pl.BlockSpec((None, CHUNK, LANES), lambda b, c: (b, c, 0)),
            pl.BlockSpec((LANES, hk), lambda b, c: (0, 0)),
            pl.BlockSpec((1, hk), lambda b, c: (0, 0)),
            pl.BlockSpec((GLA_HEADS, 1, GLA_DV), lambda b, c: (0, 0, 0)),
            pl.BlockSpec((CHUNK // PAIR_BLOCK, PAIR_BLOCK * GLA_DK, CHUNK), lambda b, c: (0, 0, 0)),
        ],
        out_specs=pl.BlockSpec((None, CHUNK, hv), lambda b, c: (b, c, 0)),
        out_shape=jax.ShapeDtypeStruct((nb, lpad, hv), BF16),
        scratch_shapes=[
            pltpu.VMEM((GLA_HEADS, GLA_DV, GLA_DK), F32),
            pltpu.VMEM((GLA_HEADS, CHUNK, GLA_DK), F32),
            pltpu.VMEM((GLA_HEADS, CHUNK, GLA_DK), F32),
            pltpu.VMEM((GLA_HEADS, CHUNK, GLA_DK), F32),
        ],
        compiler_params=_params(("parallel", "arbitrary")),
        name="gla_scan",
    )(proj, proj, proj, proj, lr, w2, b_gate, nrm, sel)


def _ret_kernel(seq_valid_end, q_ref, k_ref, v_ref, g_ref, cos_ref, sin_ref, lg_ref, nrm_ref,
                o_ref, st_ref):
    c = pl.program_id(1)

    @pl.when(c == 0)
    def _():
        st_ref[...] = jnp.zeros_like(st_ref)

    row = c * CHUNK + lax.broadcasted_iota(jnp.int32, (CHUNK, 1), 0)
    vm = jnp.where(row >= META_OFFSET, jnp.where(row < seq_valid_end, 1.0, 0.0), 0.0)
    n_i = lax.broadcasted_iota(jnp.int32, (CHUNK, CHUNK), 0).astype(F32)
    n_j = lax.broadcasted_iota(jnp.int32, (CHUNK, CHUNK), 1).astype(F32)
    dist = jnp.abs(n_i - n_j)
    n_col = lax.broadcasted_iota(jnp.int32, (CHUNK, 1), 0).astype(F32)
    cos = cos_ref[...]
    sin = sin_ref[...]
    half = RET_DK // 2

    def rot(x):
        x1 = x[:, :half]
        x2 = x[:, half:]
        return jnp.concatenate([x1 * cos - x2 * sin, x2 * cos + x1 * sin], axis=-1)

    for h in range(RET_HEADS):
        ksl = slice(h * RET_DK, (h + 1) * RET_DK)
        vsl = slice(h * RET_DV, (h + 1) * RET_DV)
        lg = lg_ref[h]
        intra = jnp.exp(dist * lg)
        xi = jnp.exp((n_col + 1.0) * lg)
        zeta = jnp.exp((CHUNK - 1.0 - n_col) * lg)
        g_chunk = jnp.exp(CHUNK * lg)
        qr = rot(q_ref[:, ksl].astype(F32)).astype(BF16)
        kr = rot(k_ref[:, ksl].astype(F32)) * (RET_DK ** -0.5) * vm
        v = v_ref[:, vsl]
        st = st_ref[h]
        s = lax.dot_general(qr, kr.astype(BF16), (((1,), (1,)), ((), ())), preferred_element_type=F32) * intra
        o = jnp.dot(s.astype(BF16), v, preferred_element_type=F32)
        o = o + lax.dot_general(qr, st.astype(BF16), (((1,), (1,)), ((), ())), preferred_element_type=F32) * xi
        st_ref[h] = st * g_chunk + lax.dot_general(v, (kr * zeta).astype(BF16), (((0,), (0,)), ((), ())),
                                                   preferred_element_type=F32)
        gv = g_ref[:, vsl].astype(F32)
        o_ref[:, vsl] = (_rms(o, nrm_ref[h]) * (gv * _sigmoid(gv))).astype(o_ref.dtype)


def _ret(proj, cos, sin, lg, nrm, lpad, seq_valid_end):
    nb = proj.shape[0]
    hk = RET_HEADS * RET_DK
    hv = RET_HEADS * RET_DV
    base = (2 * GLA_HEADS * GLA_DK + 2 * GLA_HEADS * GLA_DV) // hk
    return pl.pallas_call(
        functools.partial(_ret_kernel, seq_valid_end),
        grid=(nb, lpad // CHUNK),
        in_specs=[
            pl.BlockSpec((None, CHUNK, hk), lambda b, c: (b, c, base)),
            pl.BlockSpec((None, CHUNK, hk), lambda b, c: (b, c, base + 1)),
            pl.BlockSpec((None, CHUNK, hv), lambda b, c: (b, c, base + 2)),
            pl.BlockSpec((None, CHUNK, hv), lambda b, c: (b, c, base + 3)),
            pl.BlockSpec((CHUNK, RET_DK // 2), lambda b, c: (c, 0)),
            pl.BlockSpec((CHUNK, RET_DK // 2), lambda b, c: (c, 0)),
            pl.BlockSpec((RET_HEADS, 1, 1), lambda b, c: (0, 0, 0)),
            pl.BlockSpec((RET_HEADS, 1, RET_DV), lambda b, c: (0, 0, 0)),
        ],
        out_specs=pl.BlockSpec((None, CHUNK, hv), lambda b, c: (b, c, 0)),
        out_shape=jax.ShapeDtypeStruct((nb, lpad, hv), BF16),
        scratch_shapes=[pltpu.VMEM((RET_HEADS, RET_DV, RET_DK), F32)],
        compiler_params=_params(("parallel", "arbitrary")),
        name="ret_scan",
    )(proj, proj, proj, proj, cos, sin, lg, nrm)


def _diff_attn_kernel(seq_valid_end, lam_init, lpad, q1_ref, q2_ref, k1_ref, k2_ref, vt_ref, lam_ref, nrm_ref,
                      o_ref, acc1_ref, acc2_ref):
    qi = pl.program_id(2)
    t = ATTN_TILE
    tq = ATTN_Q_TILE
    qs = (q1_ref[...], q2_ref[...])
    k_refs = (k1_ref, k2_ref)
    acc_refs = (acc1_ref, acc2_ref)
    q_chunk = lax.shift_right_logical(qi * tq + lax.broadcasted_iota(jnp.int32, (1, tq), 1), 6)
    nt_dims = (((1,), (1,)), ((), ()))

    def start(j):
        return j * t if isinstance(j, int) else pl.multiple_of(j * t, t)

    def scores(j, masked):
        out = []
        if masked:
            k_idx = j * t + lax.broadcasted_iota(jnp.int32, (t, 1), 0)
            k_chunk = jnp.where(k_idx >= META_OFFSET,
                                jnp.where(k_idx < seq_valid_end, lax.shift_right_logical(k_idx, 6), 2**30), 2**30)
            mask = k_chunk <= q_chunk
        for b in range(2):
            s = lax.dot_general(k_refs[b][pl.ds(start(j), t), :], qs[b], nt_dims, preferred_element_type=F32)
            out.append(jnp.where(mask, s, -1e30) if masked else s)
        return tuple(out)

    def update(j, stats, s_cur):
        vt = vt_ref[:, pl.ds(start(j), t)]
        new = []
        for b in range(2):
            m, l = stats[b]
            s = s_cur[b]
            m_new = jnp.maximum(m, jnp.max(s, axis=0, keepdims=True))
            alpha = jnp.exp2(m - m_new)
            p = jnp.exp2(s - m_new)
            l_new = alpha * l + jnp.sum(p, axis=0, keepdims=True)
            acc_refs[b][...] = alpha * acc_refs[b][...] + jnp.dot(vt, p.astype(BF16), preferred_element_type=F32)
            new.append((m_new, l_new))
        return tuple(new)

    def masked_body(j, carry):
        stats, s_cur = carry
        s_next = scores(j + 1, True)
        return update(j, stats, s_cur), s_next

    def pair_body(p, carry):
        stats, s_cur = carry
        j = 2 * p
        s_mid = scores(j + 1, False)
        stats = update(j, stats, s_cur)
        s_next = scores(j + 2, False)
        return update(j + 1, stats, s_mid), s_next

    for b in range(2):
        acc_refs[b][...] = jnp.zeros_like(acc_refs[b])
    init = (jnp.full((1, tq), -1e30, F32), jnp.zeros((1, tq), F32))
    carry = ((init, init), scores(0, True))
    first_diag = qi * (tq // t)
    n_tiles = jnp.minimum(first_diag + tq // t, lpad // t)
    n_plain = jnp.maximum(first_diag - 1, 0)
    n_pairs = lax.shift_right_logical(n_plain, 1)
    carry = lax.fori_loop(0, n_pairs, pair_body, carry)
    carry = lax.fori_loop(2 * n_pairs, n_tiles - 1, masked_body, carry)
    (_, l1), (_, l2) = update(n_tiles - 1, *carry)

    lam_v = lam_ref[...]
    lam = (jnp.exp(jnp.sum(lam_v[0:1] * lam_v[1:2], axis=-1, keepdims=True))
           - jnp.exp(jnp.sum(lam_v[2:3] * lam_v[3:4], axis=-1, keepdims=True)) + lam_init)
    o_t = acc1_ref[...] * (1.0 / l1) - acc2_ref[...] * (lam * (1.0 / l2))
    o_ref[...] = (_rms(o_t.T, nrm_ref[...]) * (1.0 - lam_init)).astype(o_ref.dtype)


def _diff_attn(qk, vt, lam_vecs, nrm, lpad, seq_valid_end, lam_init):
    nb = qk.shape[0]
    t = ATTN_TILE
    tq = ATTN_Q_TILE
    kb = 2 * DIFF_HEADS
    return pl.pallas_call(
        functools.partial(_diff_attn_kernel, seq_valid_end, lam_init, lpad),
        grid=(nb, DIFF_HEADS, pl.cdiv(lpad, tq)),
        in_specs=[
            pl.BlockSpec((None, tq, DIFF_DK), lambda b, h, i: (b, i, 2 * h)),
            pl.BlockSpec((None, tq, DIFF_DK), lambda b, h, i: (b, i, 2 * h + 1)),
            pl.BlockSpec((None, lpad, DIFF_DK), lambda b, h, i: (b, 0, kb + 2 * h)),
            pl.BlockSpec((None, lpad, DIFF_DK), lambda b, h, i: (b, 0, kb + 2 * h + 1)),
            pl.BlockSpec((DIFF_DV, lpad), lambda b, h, i: (h, b)),
            pl.BlockSpec((4, DIFF_DK), lambda b, h, i: (0, 0)),
            pl.BlockSpec((1, DIFF_DV), lambda b, h, i: (0, 0)),
        ],
        out_specs=pl.BlockSpec((None, tq, DIFF_DV), lambda b, h, i: (b, i, h)),
        out_shape=jax.ShapeDtypeStruct((nb, lpad, DIFF_HEADS * DIFF_DV), BF16),
        scratch_shapes=[pltpu.VMEM((DIFF_DV, tq), F32), pltpu.VMEM((DIFF_DV, tq), F32)],
        compiler_params=_params(("parallel", "parallel", "arbitrary")),
        name="diff_attn",
    )(qk, qk, qk, qk, vt, lam_vecs, nrm)


def _ffn_kernel(x_ref, halo_ref, g_ref, wv_ref, wg_ref, cwv_ref, cwg_ref, cbv_ref, cbg_ref, wd_ref, keep_ref,
                o_ref, hn_ref, acc_ref, *u_refs):
    i = pl.program_id(0)
    j = pl.program_id(1)
    tm = x_ref.shape[0]

    @pl.when(j == 0)
    def _():
        g = g_ref[...]
        hn_ref[HALO:, :] = _rms(x_ref[...], g).astype(BF16)
        first = jnp.where(i > 0, 1.0, 0.0)
        hn_ref[:HALO, :] = (_rms(halo_ref[...], g) * first).astype(BF16)
        acc_ref[...] = jnp.zeros_like(acc_ref)

    hn = hn_ref[...]

    def conv(u_ref, cw_ref, cb_ref, sl):
        cw = cw_ref[:, sl]
        c = cb_ref[:, sl] + cw[0:1] * u_ref[pl.ds(HALO - 2, tm), :]
        c = c + cw[1:2] * u_ref[pl.ds(HALO - 1, tm), :]
        return c + cw[2:3] * u_ref[pl.ds(HALO, tm), :]

    subs = [slice(s * FF_SUB, (s + 1) * FF_SUB) for s in range(FF_TILE // FF_SUB)]
    bufs = [(u_refs[2 * s], u_refs[2 * s + 1]) for s in range(len(subs))]
    for (uv_ref, ug_ref), sl in zip(bufs, subs):
        uv_ref[...] = jnp.dot(hn, wv_ref[:, sl], preferred_element_type=F32)
        ug_ref[...] = jnp.dot(hn, wg_ref[:, sl], preferred_element_type=F32)
    total = None
    for (uv_ref, ug_ref), sl in zip(bufs, subs):
        gate = conv(ug_ref, cwg_ref, cbg_ref, sl)
        act = (conv(uv_ref, cwv_ref, cbv_ref, sl) * (gate * _sigmoid(gate))).astype(BF16)
        part = jnp.dot(act, wd_ref[sl, :], preferred_element_type=F32)
        total = part if total is None else total + part
    acc_ref[...] += total

    @pl.when(j == pl.num_programs(1) - 1)
    def _():
        o_ref[...] = x_ref[...] + acc_ref[...] * keep_ref[...]


def _ffn(x, g, w_up, conv_w, conv_b, w_down, keep):
    m, d = x.shape
    f = w_down.shape[0]
    nf = f // FF_TILE
    tm = ROW_TILE
    halo_blocks = tm // HALO
    return pl.pallas_call(
        _ffn_kernel,
        grid=(m // tm, nf),
        in_specs=[
            pl.BlockSpec((tm, d), lambda i, j: (i, 0)),
            pl.BlockSpec((None, HALO, d), lambda i, j: (jnp.maximum(i * halo_blocks - 1, 0), 0, 0)),
            pl.BlockSpec((1, d), lambda i, j: (0, 0)),
            pl.BlockSpec((d, FF_TILE), lambda i, j: (0, j)),
            pl.BlockSpec((d, FF_TILE), lambda i, j: (0, nf + j)),
            pl.BlockSpec((CONV_W, FF_TILE), lambda i, j: (0, j)),
            pl.BlockSpec((CONV_W, FF_TILE), lambda i, j: (0, nf + j)),
            pl.BlockSpec((1, FF_TILE), lambda i, j: (0, j)),
            pl.BlockSpec((1, FF_TILE), lambda i, j: (0, nf + j)),
            pl.BlockSpec((FF_TILE, d), lambda i, j: (j, 0)),
            pl.BlockSpec((tm, 1), lambda i, j: (i, 0)),
        ],
        out_specs=pl.BlockSpec((tm, d), lambda i, j: (i, 0)),
        out_shape=jax.ShapeDtypeStruct((m, d), F32),
        scratch_shapes=[pltpu.VMEM((tm + HALO, d), BF16), pltpu.VMEM((tm, d), F32)]
        + [pltpu.VMEM((tm + HALO, FF_SUB), F32)] * (2 * FF_TILE // FF_SUB),
        compiler_params=_params(("parallel", "arbitrary")),
        name="conv_ffn",
    )(x, x.reshape(m // HALO, HALO, d), g.reshape(1, d), w_up, w_up, conv_w, conv_w,
      conv_b.reshape(1, -1), conv_b.reshape(1, -1), w_down, keep)


def _rope_tables(pos, inv_freq):
    ang = pos.astype(F32)[:, None] * inv_freq[None, :]
    return jnp.cos(ang), jnp.sin(ang)


def kernel(x, meta, mix_norm_e, w_in_e, gla_w_gate_e, gla_b_gate_e, gla_norm_e, ret_norm_e, w_out_e, mix_norm_o, w_in_o, q_norm_o, k_norm_o, lam_q1_o, lam_k1_o, lam_q2_o, lam_k2_o, diff_norm_o, w_out_o, ffn_norm, w_up, conv_w, conv_b, w_down):
    nb, s, d = x.shape
    seq_valid_end = CHUNK + s
    lpad = -(-seq_valid_end // ATTN_TILE) * ATTN_TILE
    m = nb * lpad

    h = jnp.concatenate([jnp.zeros((nb, META_OFFSET, d), x.dtype),
                         jnp.broadcast_to(meta.astype(x.dtype)[None], (nb, N_META, d)),
                         x,
                         jnp.zeros((nb, lpad - seq_valid_end, d), x.dtype)], axis=1).reshape(m, d)
    idx = jnp.arange(lpad)
    valid = (idx >= META_OFFSET) & (idx < seq_valid_end)
    pos = idx - META_OFFSET
    keep = jnp.tile(valid.astype(F32), nb).reshape(m, 1)

    gla_cols = 2 * GLA_HEADS * GLA_DK + 2 * GLA_HEADS * GLA_DV
    w_in = w_in_e[0]
    w_main = jnp.concatenate([w_in[:, :gla_cols], w_in[:, gla_cols + GLA_GATE_RANK:]], axis=1).astype(BF16)
    w_lr = jnp.pad(w_in[:, gla_cols:gla_cols + GLA_GATE_RANK], ((0, 0), (0, LANES - GLA_GATE_RANK))).astype(BF16)
    w_gate2 = jnp.pad(gla_w_gate_e[0], ((0, LANES - GLA_GATE_RANK), (0, 0))).astype(BF16)
    proj = _norm_matmul(h, mix_norm_e[0], w_main, 1792).reshape(nb, lpad, -1)
    lr = _norm_matmul(h, mix_norm_e[0], w_lr, LANES).reshape(nb, lpad, LANES)
    sel = (jnp.arange(CHUNK)[:, None, None] == jnp.arange(CHUNK)[None, None, :]).astype(BF16)
    sel = jnp.broadcast_to(sel, (CHUNK, GLA_DK, CHUNK)).reshape(CHUNK // PAIR_BLOCK, PAIR_BLOCK * GLA_DK, CHUNK)
    o_a = _gla(proj, lr, w_gate2, gla_b_gate_e[0].reshape(1, -1), gla_norm_e[0].reshape(GLA_HEADS, 1, GLA_DV),
               sel, lpad, seq_valid_end)
    inv_freq_ret = 1.0 / (ROPE_THETA ** jnp.linspace(0.0, 1.0, RET_DK // 2, dtype=F32))
    cos_r, sin_r = _rope_tables(pos, inv_freq_ret)
    log_gamma = jnp.log(1.0 - 2.0 ** (-5.0 - jnp.arange(RET_HEADS, dtype=F32))).reshape(RET_HEADS, 1, 1)
    o_b = _ret(proj, cos_r, sin_r, log_gamma, ret_norm_e[0].reshape(RET_HEADS, 1, RET_DV), lpad, seq_valid_end)
    h = _matmul_res([o_a.reshape(m, -1), o_b.reshape(m, -1)], w_out_e[0].astype(BF16), h, keep)
    h = _ffn(h, ffn_norm[0], w_up[0].astype(BF16), conv_w[0], conv_b[0], w_down[0].astype(BF16), keep)

    layer = 1
    lam_init = 0.8 - 0.6 * math.exp(-0.3 * layer)
    qk_cols = 4 * DIFF_HEADS * DIFF_DK
    inv_freq = 1.0 / (ROPE_THETA ** (jnp.arange(0, DIFF_DK, 2, dtype=F32) / DIFF_DK))
    cos_d, sin_d = _rope_tables(pos, inv_freq)
    cos_full = jnp.tile(jnp.concatenate([cos_d, cos_d], axis=-1), (nb, 1))
    sin_signed = jnp.tile(jnp.concatenate([-sin_d, sin_d], axis=-1), (nb, 1))
    gains = jnp.stack([q_norm_o[0] * (DIFF_DK ** -0.5 * LOG2E), k_norm_o[0]]).reshape(2, 1, DIFF_DK)
    qk = _norm_matmul_qk(h, mix_norm_o[0], w_in_o[0][:, :qk_cols].astype(BF16), cos_full, sin_signed, gains,
                         1024).reshape(nb, lpad, qk_cols)
    vt = _norm_matmul_t(h, mix_norm_o[0], w_in_o[0][:, qk_cols:].T.astype(BF16), 2048)
    lam_vecs = jnp.stack([lam_q1_o[0], lam_k1_o[0], lam_q2_o[0], lam_k2_o[0]]).astype(F32)
    o = _diff_attn(qk, vt, lam_vecs, diff_norm_o[0].reshape(1, -1), lpad, seq_valid_end, lam_init)
    h = _matmul_res([o.reshape(m, -1)], w_out_o[0].astype(BF16), h, keep)
    h = _ffn(h, ffn_norm[1], w_up[1].astype(BF16), conv_w[1], conv_b[1], w_down[1].astype(BF16), keep)

    return h.reshape(nb, lpad, d)[:, CHUNK:CHUNK + s]
```

```python
import functools
import math

import jax
import jax.numpy as jnp
from jax import lax
from jax.experimental import pallas as pl
from jax.experimental.pallas import tpu as pltpu

F32 = jnp.float32
BF16 = jnp.bfloat16

D_MODEL = 2048
CHUNK = 64
N_META = 16
META_OFFSET = CHUNK - N_META
EPS = 1e-6
ROPE_THETA = 10000.0

GLA_HEADS = 4
GLA_DK = 128
GLA_DV = 256
GLA_GATE_RANK = 16
GLA_GATE_NORM = 16.0

RET_HEADS = 4
RET_DK = 256
RET_DV = 256

DIFF_HEADS = 8
DIFF_DK = 128
DIFF_DV = 256

D_FF = 5632
CONV_W = 3

LANES = 128
ATTN_TILE = 256
ATTN_Q_TILE = 512
ROW_TILE = 512
PROJ_TILE = 512
HALO = 16
FF_TILE = 512
FF_SUB = 256
VMEM_LIMIT = 56 * 2**20
PAIR_BLOCK = 8
LOG2E = math.log2(math.e)


def _params(semantics):
    return pltpu.CompilerParams(dimension_semantics=semantics, vmem_limit_bytes=VMEM_LIMIT)


def _rms(x, g):
    return x * lax.rsqrt(jnp.mean(x * x, axis=-1, keepdims=True) + EPS) * g


def _sigmoid(x):
    return 1.0 / (1.0 + jnp.exp(-x))


def _norm_matmul_kernel(x_ref, g_ref, w_ref, o_ref, xn_ref):
    @pl.when(pl.program_id(1) == 0)
    def _():
        xn_ref[...] = _rms(x_ref[...], g_ref[...]).astype(BF16)

    o_ref[...] = jnp.dot(xn_ref[...], w_ref[...], preferred_element_type=F32).astype(o_ref.dtype)


def _norm_matmul(x, g, w, tn):
    m, d = x.shape
    n = w.shape[1]
    return pl.pallas_call(
        _norm_matmul_kernel,
        grid=(m // PROJ_TILE, n // tn),
        in_specs=[
            pl.BlockSpec((PROJ_TILE, d), lambda i, j: (i, 0)),
            pl.BlockSpec((1, d), lambda i, j: (0, 0)),
            pl.BlockSpec((d, tn), lambda i, j: (0, j)),
        ],
        out_specs=pl.BlockSpec((PROJ_TILE, tn), lambda i, j: (i, j)),
        out_shape=jax.ShapeDtypeStruct((m, n), BF16),
        scratch_shapes=[pltpu.VMEM((PROJ_TILE, d), BF16)],
        compiler_params=_params(("parallel", "arbitrary")),
        name="norm_matmul",
    )(x, g.reshape(1, d), w)


def _norm_matmul_qk_kernel(x_ref, g_ref, w_ref, cos_ref, sin_ref, gain_ref, o_ref, xn_ref):
    @pl.when(pl.program_id(1) == 0)
    def _():
        xn_ref[...] = _rms(x_ref[...], g_ref[...]).astype(BF16)

    y = jnp.dot(xn_ref[...], w_ref[...], preferred_element_type=F32)
    cos = cos_ref[...]
    sin = sin_ref[...]
    gain = gain_ref[...]
    for s in range(y.shape[1] // DIFF_DK):
        sl = slice(s * DIFF_DK, (s + 1) * DIFF_DK)
        ys = _rms(y[:, sl], gain)
        o_ref[s] = (ys * cos + pltpu.roll(ys, DIFF_DK // 2, axis=1) * sin).astype(o_ref.dtype)


def _norm_matmul_qk(x, g, w, cos, sin, gains, tn):
    m, d = x.shape
    n = w.shape[1]
    tiles_per_gain = n // (2 * tn)
    return pl.pallas_call(
        _norm_matmul_qk_kernel,
        grid=(m // PROJ_TILE, n // tn),
        in_specs=[
            pl.BlockSpec((PROJ_TILE, d), lambda i, j: (i, 0)),
            pl.BlockSpec((1, d), lambda i, j: (0, 0)),
            pl.BlockSpec((d, tn), lambda i, j: (0, j)),
            pl.BlockSpec((PROJ_TILE, DIFF_DK), lambda i, j: (i, 0)),
            pl.BlockSpec((PROJ_TILE, DIFF_DK), lambda i, j: (i, 0)),
            pl.BlockSpec((None, 1, DIFF_DK), lambda i, j: (j // tiles_per_gain, 0, 0)),
        ],
        out_specs=pl.BlockSpec((tn // DIFF_DK, PROJ_TILE, DIFF_DK), lambda i, j: (j, i, 0)),
        out_shape=jax.ShapeDtypeStruct((n // DIFF_DK, m, DIFF_DK), BF16),
        scratch_shapes=[pltpu.VMEM((PROJ_TILE, d), BF16)],
        compiler_params=_params(("parallel", "arbitrary")),
        name="norm_matmul_qk",
    )(x, g.reshape(1, d), w, cos, sin, gains)


def _norm_matmul_t_kernel(x_ref, g_ref, wt_ref, o_ref, xn_ref):
    @pl.when(pl.program_id(1) == 0)
    def _():
        xn_ref[...] = _rms(x_ref[...], g_ref[...]).astype(BF16)

    o_ref[...] = lax.dot_general(wt_ref[...], xn_ref[...], (((1,), (1,)), ((), ())),
                                 preferred_element_type=F32).astype(o_ref.dtype)


def _norm_matmul_t(x, g, wt, tn):
    m, d = x.shape
    n = wt.shape[0]
    return pl.pallas_call(
        _norm_matmul_t_kernel,
        grid=(m // PROJ_TILE, n // tn),
        in_specs=[
            pl.BlockSpec((PROJ_TILE, d), lambda i, j: (i, 0)),
            pl.BlockSpec((1, d), lambda i, j: (0, 0)),
            pl.BlockSpec((tn, d), lambda i, j: (j, 0)),
        ],
        out_specs=pl.BlockSpec((tn, PROJ_TILE), lambda i, j: (j, i)),
        out_shape=jax.ShapeDtypeStruct((n, m), BF16),
        scratch_shapes=[pltpu.VMEM((PROJ_TILE, d), BF16)],
        compiler_params=_params(("parallel", "arbitrary")),
        name="norm_matmul_t",
    )(x, g.reshape(1, d), wt)


def _matmul_res_kernel(n_in, *refs):
    a_refs = refs[:n_in]
    w_ref, h_ref, keep_ref, o_ref = refs[n_in:]
    mix = None
    k0 = 0
    for a_ref in a_refs:
        k1 = k0 + a_ref.shape[1]
        part = jnp.dot(a_ref[...], w_ref[k0:k1, :], preferred_element_type=F32)
        mix = part if mix is None else mix + part
        k0 = k1
    o_ref[...] = h_ref[...] + mix * keep_ref[...]


def _matmul_res(a_parts, w, h, keep):
    m = h.shape[0]
    k, n = w.shape
    return pl.pallas_call(
        functools.partial(_matmul_res_kernel, len(a_parts)),
        grid=(m // ROW_TILE,),
        in_specs=[pl.BlockSpec((ROW_TILE, a.shape[1]), lambda i: (i, 0)) for a in a_parts] + [
            pl.BlockSpec((k, n), lambda i: (0, 0)),
            pl.BlockSpec((ROW_TILE, n), lambda i: (i, 0)),
            pl.BlockSpec((ROW_TILE, 1), lambda i: (i, 0)),
        ],
        out_specs=pl.BlockSpec((ROW_TILE, n), lambda i: (i, 0)),
        out_shape=jax.ShapeDtypeStruct((m, n), F32),
        compiler_params=_params(("parallel",)),
        name="matmul_res",
    )(*a_parts, w, h, keep)


def _gla_kernel(seq_valid_end, q_ref, k_ref, v_ref, g_ref, lr_ref, w2_ref, b_ref, nrm_ref, sel_ref,
                o_ref, st_ref, cum_ref, qs_ref, ks_ref):
    c = pl.program_id(1)

    @pl.when(c == 0)
    def _():
        st_ref[...] = jnp.zeros_like(st_ref)

    row = c * CHUNK + lax.broadcasted_iota(jnp.int32, (CHUNK, 1), 0)
    vm = jnp.where(row >= META_OFFSET, jnp.where(row < seq_valid_end, 1.0, 0.0), 0.0)
    ii = lax.broadcasted_iota(jnp.int32, (CHUNK, CHUNK), 0)
    jj = lax.broadcasted_iota(jnp.int32, (CHUNK, CHUNK), 1)
    tril = jnp.where(jj <= ii, 1.0, 0.0).astype(F32)

    gate = jnp.dot(lr_ref[...], w2_ref[...], preferred_element_type=F32) + b_ref[...]
    log_a = (jnp.minimum(gate, 0.0) - jnp.log(1.0 + jnp.exp(-jnp.abs(gate)))) * (1.0 / GLA_GATE_NORM) * vm

    for h in range(GLA_HEADS):
        ksl = slice(h * GLA_DK, (h + 1) * GLA_DK)
        cum_ref[h] = jnp.dot(tril, log_a[:, ksl], preferred_element_type=F32,
                             precision=lax.Precision.HIGHEST) * LOG2E
        qs_ref[h] = q_ref[:, ksl].astype(F32) * (GLA_DK ** -0.5)
        ks_ref[h] = k_ref[:, ksl].astype(F32) * vm

    def decay(cum, cj, r0):
        r1 = r0 + PAIR_BLOCK
        parts = [jnp.exp2(-jnp.abs(cum[r0:r1] - cj))]
        if r0 > 0:
            parts.insert(0, jnp.exp2(cj - cum[:r0]))
        if r1 < CHUNK:
            parts.append(jnp.exp2(cum[r1:] - cj))
        return jnp.concatenate(parts, axis=0)

    scores = []
    for h in range(GLA_HEADS):
        cum = cum_ref[h]
        q = qs_ref[h]
        total = None
        for jb in range(CHUNK // PAIR_BLOCK):
            ws = []
            for j in range(jb * PAIR_BLOCK, (jb + 1) * PAIR_BLOCK):
                cj = cum_ref[h, j:j + 1, :]
                kj = ks_ref[h, j:j + 1, :]
                ws.append((decay(cum, cj, jb * PAIR_BLOCK) * (q * kj)).astype(BF16))
            part = jnp.dot(jnp.concatenate(ws, axis=1), sel_ref[jb], preferred_element_type=F32)
            total = part if total is None else total + part
        scores.append(total)

    for h in range(GLA_HEADS):
        vsl = slice(h * GLA_DV, (h + 1) * GLA_DV)
        cum = cum_ref[h]
        v = v_ref[:, vsl]
        st = st_ref[h]
        q_in = (qs_ref[h] * jnp.exp2(cum)).astype(BF16)
        o = jnp.dot(scores[h].astype(BF16), v, preferred_element_type=F32)
        o = o + lax.dot_general(q_in, st.astype(BF16), (((1,), (1,)), ((), ())), preferred_element_type=F32)
        last = cum[CHUNK - 1:CHUNK, :]
        k_dec = (ks_ref[h] * jnp.exp2(last - cum)).astype(BF16)
        st_ref[h] = st * jnp.exp2(last) + lax.dot_general(v, k_dec, (((0,), (0,)), ((), ())),
                                                          preferred_element_type=F32)
        gv = g_ref[:, vsl].astype(F32)
        o_ref[:, vsl] = (_rms(o, nrm_ref[h]) * (gv * _sigmoid(gv))).astype(o_ref.dtype)


def _gla(proj, lr, w2, b_gate, nrm, sel, lpad, seq_valid_end):
    nb = proj.shape[0]
    hk = GLA_HEADS * GLA_DK
    hv = GLA_HEADS * GLA_DV
    return pl.pallas_call(
        functools.partial(_gla_kernel, seq_valid_end),
        grid=(nb, lpad // CHUNK),
        in_specs=[
            pl.BlockSpec((None, CHUNK, hk), lambda b, c: (b, c, 0)),
            pl.BlockSpec((None, CHUNK, hk), lambda b, c: (b, c, 1)),
            pl.BlockSpec((None, CHUNK, hv), lambda b, c: (b, c, 1)),
            pl.BlockSpec((None, CHUNK, hv), lambda b, c: (b, c, 2)),
            pl.BlockSpec((None, CHUNK, LANES), lambda b, c: (b, c, 0)),
            pl.BlockSpec((LANES, hk), lambda b, c: (0, 0)),
            pl.BlockSpec((1, hk), lambda b, c: (0, 0)),
            pl.BlockSpec((GLA_HEADS, 1, GLA_DV), lambda b, c: (0, 0, 0)),
            pl.BlockSpec((CHUNK // PAIR_BLOCK, PAIR_BLOCK * GLA_DK, CHUNK), lambda b, c: (0, 0, 0)),
        ],
        out_specs=pl.BlockSpec((None, CHUNK, hv), lambda b, c: (b, c, 0)),
        out_shape=jax.ShapeDtypeStruct((nb, lpad, hv), BF16),
        scratch_shapes=[
            pltpu.VMEM((GLA_HEADS, GLA_DV, GLA_DK), F32),
            pltpu.VMEM((GLA_HEADS, CHUNK, GLA_DK), F32),
            pltpu.VMEM((GLA_HEADS, CHUNK, GLA_DK), F32),
            pltpu.VMEM((GLA_HEADS, CHUNK, GLA_DK), F32),
        ],
        compiler_params=_params(("parallel", "arbitrary")),
        name="gla_scan",
    )(proj, proj, proj, proj, lr, w2, b_gate, nrm, sel)


def _ret_kernel(seq_valid_end, q_ref, k_ref, v_ref, g_ref, cos_ref, sin_ref, lg_ref, nrm_ref,
                o_ref, st_ref):
    c = pl.program_id(1)

    @pl.when(c == 0)
    def _():
        st_ref[...] = jnp.zeros_like(st_ref)

    row = c * CHUNK + lax.broadcasted_iota(jnp.int32, (CHUNK, 1), 0)
    vm = jnp.where(row >= META_OFFSET, jnp.where(row < seq_valid_end, 1.0, 0.0), 0.0)
    n_i = lax.broadcasted_iota(jnp.int32, (CHUNK, CHUNK), 0).astype(F32)
    n_j = lax.broadcasted_iota(jnp.int32, (CHUNK, CHUNK), 1).astype(F32)
    dist = jnp.abs(n_i - n_j)
    n_col = lax.broadcasted_iota(jnp.int32, (CHUNK, 1), 0).astype(F32)
    cos = cos_ref[...]
    sin = sin_ref[...]
    half = RET_DK // 2

    def rot(x):
        x1 = x[:, :half]
        x2 = x[:, half:]
        return jnp.concatenate([x1 * cos - x2 * sin, x2 * cos + x1 * sin], axis=-1)

    for h in range(RET_HEADS):
        ksl = slice(h * RET_DK, (h + 1) * RET_DK)
        vsl = slice(h * RET_DV, (h + 1) * RET_DV)
        lg = lg_ref[h]
        intra = jnp.exp(dist * lg)
        xi = jnp.exp((n_col + 1.0) * lg)
        zeta = jnp.exp((CHUNK - 1.0 - n_col) * lg)
        g_chunk = jnp.exp(CHUNK * lg)
        qr = rot(q_ref[:, ksl].astype(F32)).astype(BF16)
        kr = rot(k_ref[:, ksl].astype(F32)) * (RET_DK ** -0.5) * vm
        v = v_ref[:, vsl]
        st = st_ref[h]
        s = lax.dot_general(qr, kr.astype(BF16), (((1,), (1,)), ((), ())), preferred_element_type=F32) * intra
        o = jnp.dot(s.astype(BF16), v, preferred_element_type=F32)
        o = o + lax.dot_general(qr, st.astype(BF16), (((1,), (1,)), ((), ())), preferred_element_type=F32) * xi
        st_ref[h] = st * g_chunk + lax.dot_general(v, (kr * zeta).astype(BF16), (((0,), (0,)), ((), ())),
                                                   preferred_element_type=F32)
        gv = g_ref[:, vsl].astype(F32)
        o_ref[:, vsl] = (_rms(o, nrm_ref[h]) * (gv * _sigmoid(gv))).astype(o_ref.dtype)


def _ret(proj, cos, sin, lg, nrm, lpad, seq_valid_end):
    nb = proj.shape[0]
    hk = RET_HEADS * RET_DK
    hv = RET_HEADS * RET_DV
    base = (2 * GLA_HEADS * GLA_DK + 2 * GLA_HEADS * GLA_DV) // hk
    return pl.pallas_call(
        functools.partial(_ret_kernel, seq_valid_end),
        grid=(nb, lpad // CHUNK),
        in_specs=[
            pl.BlockSpec((None, CHUNK, hk), lambda b, c: (b, c, base)),
            pl.BlockSpec((None, CHUNK, hk), lambda b, c: (b, c, base + 1)),
            pl.BlockSpec((None, CHUNK, hv), lambda b, c: (b, c, base + 2)),
            pl.BlockSpec((None, CHUNK, hv), lambda b, c: (b, c, base + 3)),
            pl.BlockSpec((CHUNK, RET_DK // 2), lambda b, c: (c, 0)),
            pl.BlockSpec((CHUNK, RET_DK // 2), lambda b, c: (c, 0)),
            pl.BlockSpec((RET_HEADS, 1, 1), lambda b, c: (0, 0, 0)),
            pl.BlockSpec((RET_HEADS, 1, RET_DV), lambda b, c: (0, 0, 0)),
        ],
        out_specs=pl.BlockSpec((None, CHUNK, hv), lambda b, c: (b, c, 0)),
        out_shape=jax.ShapeDtypeStruct((nb, lpad, hv), BF16),
        scratch_shapes=[pltpu.VMEM((RET_HEADS, RET_DV, RET_DK), F32)],
        compiler_params=_params(("parallel", "arbitrary")),
        name="ret_scan",
    )(proj, proj, proj, proj, cos, sin, lg, nrm)


def _diff_attn_kernel(seq_valid_end, lam_init, lpad, q1_ref, q2_ref, k1_ref, k2_ref, vt_ref, lam_ref, nrm_ref,
                      o_ref, acc1_ref, acc2_ref):
    qi = pl.program_id(2)
    t = ATTN_TILE
    tq = ATTN_Q_TILE
    qs = (q1_ref[...], q2_ref[...])
    k_refs = (k1_ref, k2_ref)
    acc_refs = (acc1_ref, acc2_ref)
    q_chunk = lax.shift_right_logical(qi * tq + lax.broadcasted_iota(jnp.int32, (1, tq), 1), 6)
    nt_dims = (((1,), (1,)), ((), ()))

    def start(j):
        return j * t if isinstance(j, int) else pl.multiple_of(j * t, t)

    def scores(j, masked):
        out = []
        if masked:
            k_idx = j * t + lax.broadcasted_iota(jnp.int32, (t, 1), 0)
            k_chunk = jnp.where(k_idx >= META_OFFSET,
                                jnp.where(k_idx < seq_valid_end, lax.shift_right_logical(k_idx, 6), 2**30), 2**30)
            mask = k_chunk <= q_chunk
        for b in range(2):
            s = lax.dot_general(k_refs[b][pl.ds(start(j), t), :], qs[b], nt_dims, preferred_element_type=F32)
            out.append(jnp.where(mask, s, -1e30) if masked else s)
        return tuple(out)

    def update(j, stats, s_cur):
        vt = vt_ref[:, pl.ds(start(j), t)]
        new = []
        for b in range(2):
            m, l = stats[b]
            s = s_cur[b]
            m_new = jnp.maximum(m, jnp.max(s, axis=0, keepdims=True))
            alpha = jnp.exp2(m - m_new)
            p = jnp.exp2(s - m_new)
            l_new = alpha * l + jnp.sum(p, axis=0, keepdims=True)
            acc_refs[b][...] = alpha * acc_refs[b][...] + jnp.dot(vt, p.astype(BF16), preferred_element_type=F32)
            new.append((m_new, l_new))
        return tuple(new)

    def masked_body(j, carry):
        stats, s_cur = carry
        s_next = scores(j + 1, True)
        return update(j, stats, s_cur), s_next

    def pair_body(p, carry):
        stats, s_cur = carry
        j = 2 * p
        s_mid = scores(j + 1, False)
        stats = update(j, stats, s_cur)
        s_next = scores(j + 2, False)
        return update(j + 1, stats, s_mid), s_next

    for b in range(2):
        acc_refs[b][...] = jnp.zeros_like(acc_refs[b])
    init = (jnp.full((1, tq), -1e30, F32), jnp.zeros((1, tq), F32))
    carry = ((init, init), scores(0, True))
    first_diag = qi * (tq // t)
    n_tiles = jnp.minimum(first_diag + tq // t, lpad // t)
    n_plain = jnp.maximum(first_diag - 1, 0)
    n_pairs = lax.shift_right_logical(n_plain, 1)
    carry = lax.fori_loop(0, n_pairs, pair_body, carry)
    carry = lax.fori_loop(2 * n_pairs, n_tiles - 1, masked_body, carry)
    (_, l1), (_, l2) = update(n_tiles - 1, *carry)

    lam_v = lam_ref[...]
    lam = (jnp.exp(jnp.sum(lam_v[0:1] * lam_v[1:2], axis=-1, keepdims=True))
           - jnp.exp(jnp.sum(lam_v[2:3] * lam_v[3:4], axis=-1, keepdims=True)) + lam_init)
    o_t = acc1_ref[...] * (1.0 / l1) - acc2_ref[...] * (lam * (1.0 / l2))
    o_ref[...] = (_rms(o_t.T, nrm_ref[...]) * (1.0 - lam_init)).astype(o_ref.dtype)


def _diff_attn(qk, vt, lam_vecs, nrm, lpad, seq_valid_end, lam_init):
    nb = qk.shape[1]
    t = ATTN_TILE
    tq = ATTN_Q_TILE
    kb = 2 * DIFF_HEADS
    return pl.pallas_call(
        functools.partial(_diff_attn_kernel, seq_valid_end, lam_init, lpad),
        grid=(nb, DIFF_HEADS, pl.cdiv(lpad, tq)),
        in_specs=[
            pl.BlockSpec((None, None, tq, DIFF_DK), lambda b, h, i: (2 * h, b, i, 0)),
            pl.BlockSpec((None, None, tq, DIFF_DK), lambda b, h, i: (2 * h + 1, b, i, 0)),
            pl.BlockSpec((None, None, lpad, DIFF_DK), lambda b, h, i: (kb + 2 * h, b, 0, 0)),
            pl.BlockSpec((None, None, lpad, DIFF_DK), lambda b, h, i: (kb + 2 * h + 1, b, 0, 0)),
            pl.BlockSpec((DIFF_DV, lpad), lambda b, h, i: (h, b)),
            pl.BlockSpec((4, DIFF_DK), lambda b, h, i: (0, 0)),
            pl.BlockSpec((1, DIFF_DV), lambda b, h, i: (0, 0)),
        ],
        out_specs=pl.BlockSpec((None, tq, DIFF_DV), lambda b, h, i: (b, i, h)),
        out_shape=jax.ShapeDtypeStruct((nb, lpad, DIFF_HEADS * DIFF_DV), BF16),
        scratch_shapes=[pltpu.VMEM((DIFF_DV, tq), F32), pltpu.VMEM((DIFF_DV, tq), F32)],
        compiler_params=_params(("parallel", "parallel", "arbitrary")),
        name="diff_attn",
    )(qk, qk, qk, qk, vt, lam_vecs, nrm)


def _ffn_kernel(x_ref, halo_ref, g_ref, wv_ref, wg_ref, cwv_ref, cwg_ref, cbv_ref, cbg_ref, wd_ref, keep_ref,
                o_ref, hn_ref, acc_ref, *u_refs):
    i = pl.program_id(0)
    j = pl.program_id(1)
    tm = x_ref.shape[0]

    @pl.when(j == 0)
    def _():
        g = g_ref[...]
        hn_ref[HALO:, :] = _rms(x_ref[...], g).astype(BF16)
        first = jnp.where(i > 0, 1.0, 0.0)
        hn_ref[:HALO, :] = (_rms(halo_ref[...], g) * first).astype(BF16)
        acc_ref[...] = jnp.zeros_like(acc_ref)

    hn = hn_ref[...]

    def conv(u_ref, cw_ref, cb_ref, sl):
        cw = cw_ref[:, sl]
        c = cb_ref[:, sl] + cw[0:1] * u_ref[pl.ds(HALO - 2, tm), :]
        c = c + cw[1:2] * u_ref[pl.ds(HALO - 1, tm), :]
        return c + cw[2:3] * u_ref[pl.ds(HALO, tm), :]

    subs = [slice(s * FF_SUB, (s + 1) * FF_SUB) for s in range(FF_TILE // FF_SUB)]
    bufs = [(u_refs[2 * s], u_refs[2 * s + 1]) for s in range(len(subs))]
    for (uv_ref, ug_ref), sl in zip(bufs, subs):
        uv_ref[...] = jnp.dot(hn, wv_ref[:, sl], preferred_element_type=F32)
        ug_ref[...] = jnp.dot(hn, wg_ref[:, sl], preferred_element_type=F32)
    total = None
    for (uv_ref, ug_ref), sl in zip(bufs, subs):
        gate = conv(ug_ref, cwg_ref, cbg_ref, sl)
        act = (conv(uv_ref, cwv_ref, cbv_ref, sl) * (gate * _sigmoid(gate))).astype(BF16)
        part = jnp.dot(act, wd_ref[sl, :], preferred_element_type=F32)
        total = part if total is None else total + part
    acc_ref[...] += total

    @pl.when(j == pl.num_programs(1) - 1)
    def _():
        o_ref[...] = x_ref[...] + acc_ref[...] * keep_ref[...]


def _ffn(x, g, w_up, conv_w, conv_b, w_down, keep):
    m, d = x.shape
    f = w_down.shape[0]
    nf = f // FF_TILE
    tm = ROW_TILE
    halo_blocks = tm // HALO
    return pl.pallas_call(
        _ffn_kernel,
        grid=(m // tm, nf),
        in_specs=[
            pl.BlockSpec((tm, d), lambda i, j: (i, 0)),
            pl.BlockSpec((None, HALO, d), lambda i, j: (jnp.maximum(i * halo_blocks - 1, 0), 0, 0)),
            pl.BlockSpec((1, d), lambda i, j: (0, 0)),
            pl.BlockSpec((d, FF_TILE), lambda i, j: (0, j)),
            pl.BlockSpec((d, FF_TILE), lambda i, j: (0, nf + j)),
            pl.BlockSpec((CONV_W, FF_TILE), lambda i, j: (0, j)),
            pl.BlockSpec((CONV_W, FF_TILE), lambda i, j: (0, nf + j)),
            pl.BlockSpec((1, FF_TILE), lambda i, j: (0, j)),
            pl.BlockSpec((1, FF_TILE), lambda i, j: (0, nf + j)),
            pl.BlockSpec((FF_TILE, d), lambda i, j: (j, 0)),
            pl.BlockSpec((tm, 1), lambda i, j: (i, 0)),
        ],
        out_specs=pl.BlockSpec((tm, d), lambda i, j: (i, 0)),
        out_shape=jax.ShapeDtypeStruct((m, d), F32),
        scratch_shapes=[pltpu.VMEM((tm + HALO, d), BF16), pltpu.VMEM((tm, d), F32)]
        + [pltpu.VMEM((tm + HALO, FF_SUB), F32)] * (2 * FF_TILE // FF_SUB),
        compiler_params=_params(("parallel", "arbitrary")),
        name="conv_ffn",
    )(x, x.reshape(m // HALO, HALO, d), g.reshape(1, d), w_up, w_up, conv_w, conv_w,
      conv_b.reshape(1, -1), conv_b.reshape(1, -1), w_down, keep)


def _rope_tables(pos, inv_freq):
    ang = pos.astype(F32)[:, None] * inv_freq[None, :]
    return jnp.cos(ang), jnp.sin(ang)


def kernel(x, meta, mix_norm_e, w_in_e, gla_w_gate_e, gla_b_gate_e, gla_norm_e, ret_norm_e, w_out_e, mix_norm_o, w_in_o, q_norm_o, k_norm_o, lam_q1_o, lam_k1_o, lam_q2_o, lam_k2_o, diff_norm_o, w_out_o, ffn_norm, w_up, conv_w, conv_b, w_down):
    nb, s, d = x.shape
    seq_valid_end = CHUNK + s
    lpad = -(-seq_valid_end // ATTN_TILE) * ATTN_TILE
    m = nb * lpad

    h = jnp.concatenate([jnp.zeros((nb, META_OFFSET, d), x.dtype),
                         jnp.broadcast_to(meta.astype(x.dtype)[None], (nb, N_META, d)),
                         x,
                         jnp.zeros((nb, lpad - seq_valid_end, d), x.dtype)], axis=1).reshape(m, d)
    idx = jnp.arange(lpad)
    valid = (idx >= META_OFFSET) & (idx < seq_valid_end)
    pos = idx - META_OFFSET
    keep = jnp.tile(valid.astype(F32), nb).reshape(m, 1)

    gla_cols = 2 * GLA_HEADS * GLA_DK + 2 * GLA_HEADS * GLA_DV
    w_in = w_in_e[0]
    w_main = jnp.concatenate([w_in[:, :gla_cols], w_in[:, gla_cols + GLA_GATE_RANK:]], axis=1).astype(BF16)
    w_lr = jnp.pad(w_in[:, gla_cols:gla_cols + GLA_GATE_RANK], ((0, 0), (0, LANES - GLA_GATE_RANK))).astype(BF16)
    w_gate2 = jnp.pad(gla_w_gate_e[0], ((0, LANES - GLA_GATE_RANK), (0, 0))).astype(BF16)
    proj = _norm_matmul(h, mix_norm_e[0], w_main, 3584).reshape(nb, lpad, -1)
    lr = _norm_matmul(h, mix_norm_e[0], w_lr, LANES).reshape(nb, lpad, LANES)
    sel = (jnp.arange(CHUNK)[:, None, None] == jnp.arange(CHUNK)[None, None, :]).astype(BF16)
    sel = jnp.broadcast_to(sel, (CHUNK, GLA_DK, CHUNK)).reshape(CHUNK // PAIR_BLOCK, PAIR_BLOCK * GLA_DK, CHUNK)
    o_a = _gla(proj, lr, w_gate2, gla_b_gate_e[0].reshape(1, -1), gla_norm_e[0].reshape(GLA_HEADS, 1, GLA_DV),
               sel, lpad, seq_valid_end)
    inv_freq_ret = 1.0 / (ROPE_THETA ** jnp.linspace(0.0, 1.0, RET_DK // 2, dtype=F32))
    cos_r, sin_r = _rope_tables(pos, inv_freq_ret)
    log_gamma = jnp.log(1.0 - 2.0 ** (-5.0 - jnp.arange(RET_HEADS, dtype=F32))).reshape(RET_HEADS, 1, 1)
    o_b = _ret(proj, cos_r, sin_r, log_gamma, ret_norm_e[0].reshape(RET_HEADS, 1, RET_DV), lpad, seq_valid_end)
    h = _matmul_res([o_a.reshape(m, -1), o_b.reshape(m, -1)], w_out_e[0].astype(BF16), h, keep)
    h = _ffn(h, ffn_norm[0], w_up[0].astype(BF16), conv_w[0], conv_b[0], w_down[0].astype(BF16), keep)

    layer = 1
    lam_init = 0.8 - 0.6 * math.exp(-0.3 * layer)
    qk_cols = 4 * DIFF_HEADS * DIFF_DK
    inv_freq = 1.0 / (ROPE_THETA ** (jnp.arange(0, DIFF_DK, 2, dtype=F32) / DIFF_DK))
    cos_d, sin_d = _rope_tables(pos, inv_freq)
    cos_full = jnp.tile(jnp.concatenate([cos_d, cos_d], axis=-1), (nb, 1))
    sin_signed = jnp.tile(jnp.concatenate([-sin_d, sin_d], axis=-1), (nb, 1))
    gains = jnp.stack([q_norm_o[0] * (DIFF_DK ** -0.5 * LOG2E), k_norm_o[0]]).reshape(2, 1, DIFF_DK)
    qk = _norm_matmul_qk(h, mix_norm_o[0], w_in_o[0][:, :qk_cols].astype(BF16), cos_full, sin_signed, gains,
                         2048).reshape(qk_cols // DIFF_DK, nb, lpad, DIFF_DK)
    vt = _norm_matmul_t(h, mix_norm_o[0], w_in_o[0][:, qk_cols:].T.astype(BF16), 2048)
    lam_vecs = jnp.stack([lam_q1_o[0], lam_k1_o[0], lam_q2_o[0], lam_k2_o[0]]).astype(F32)
    o = _diff_attn(qk, vt, lam_vecs, diff_norm_o[0].reshape(1, -1), lpad, seq_valid_end, lam_init)
    h = _matmul_res([o.reshape(m, -1)], w_out_o[0].astype(BF16), h, keep)
    h = _ffn(h, ffn_norm[1], w_up[1].astype(BF16), conv_w[1], conv_b[1], w_down[1].astype(BF16), keep)

    return h.reshape(nb, lpad, d)[:, CHUNK:CHUNK + s]
```

```python
import functools
import math

import jax
import jax.numpy as jnp
from jax import lax
from jax.experimental import pallas as pl
from jax.experimental.pallas import tpu as pltpu

F32 = jnp.float32
BF16 = jnp.bfloat16

D_MODEL = 2048
CHUNK = 64
N_META = 16
META_OFFSET = CHUNK - N_META
EPS = 1e-6
ROPE_THETA = 10000.0

GLA_HEADS = 4
GLA_DK = 128
GLA_DV = 256
GLA_GATE_RANK = 16
GLA_GATE_NORM = 16.0

RET_HEADS = 4
RET_DK = 256
RET_DV = 256

DIFF_HEADS = 8
DIFF_DK = 128
DIFF_DV = 256

D_FF = 5632
CONV_W = 3

LANES = 128
ATTN_TILE = 256
ATTN_Q_TILE = 512
ROW_TILE = 512
PROJ_TILE = 512
HALO = 16
FF_TILE = 512
FF_SUB = 256
VMEM_LIMIT = 56 * 2**20
PAIR_BLOCK = 8
LOG2E = math.log2(math.e)


def _params(semantics):
    return pltpu.CompilerParams(dimension_semantics=semantics, vmem_limit_bytes=VMEM_LIMIT)


def _rms(x, g):
    return x * lax.rsqrt(jnp.mean(x * x, axis=-1, keepdims=True) + EPS) * g


def _sigmoid(x):
    return 1.0 / (1.0 + jnp.exp(-x))


def _norm_matmul_kernel(x_ref, g_ref, w_ref, o_ref, xn_ref):
    @pl.when(pl.program_id(1) == 0)
    def _():
        xn_ref[...] = _rms(x_ref[...], g_ref[...]).astype(BF16)

    o_ref[...] = jnp.dot(xn_ref[...], w_ref[...], preferred_element_type=F32).astype(o_ref.dtype)


def _norm_matmul(x, g, w, tn):
    m, d = x.shape
    n = w.shape[1]
    return pl.pallas_call(
        _norm_matmul_kernel,
        grid=(m // PROJ_TILE, n // tn),
        in_specs=[
            pl.BlockSpec((PROJ_TILE, d), lambda i, j: (i, 0)),
            pl.BlockSpec((1, d), lambda i, j: (0, 0)),
            pl.BlockSpec((d, tn), lambda i, j: (0, j)),
        ],
        out_specs=pl.BlockSpec((PROJ_TILE, tn), lambda i, j: (i, j)),
        out_shape=jax.ShapeDtypeStruct((m, n), BF16),
        scratch_shapes=[pltpu.VMEM((PROJ_TILE, d), BF16)],
        compiler_params=_params(("parallel", "arbitrary")),
        name="norm_matmul",
    )(x, g.reshape(1, d), w)


def _norm_matmul_qk_kernel(x_ref, g_ref, w_ref, cos_ref, sin_ref, gain_ref, o_ref, xn_ref):
    @pl.when(pl.program_id(1) == 0)
    def _():
        xn_ref[...] = _rms(x_ref[...], g_ref[...]).astype(BF16)

    y = jnp.dot(xn_ref[...], w_ref[...], preferred_element_type=F32)
    cos = cos_ref[...]
    sin = sin_ref[...]
    gain = gain_ref[...]
    for s in range(y.shape[1] // DIFF_DK):
        sl = slice(s * DIFF_DK, (s + 1) * DIFF_DK)
        ys = _rms(y[:, sl], gain)
        o_ref[s] = (ys * cos + pltpu.roll(ys, DIFF_DK // 2, axis=1) * sin).astype(o_ref.dtype)


def _norm_matmul_qk(x, g, w, cos, sin, gains, tn):
    m, d = x.shape
    n = w.shape[1]
    tiles_per_gain = n // (2 * tn)
    return pl.pallas_call(
        _norm_matmul_qk_kernel,
        grid=(m // PROJ_TILE, n // tn),
        in_specs=[
            pl.BlockSpec((PROJ_TILE, d), lambda i, j: (i, 0)),
            pl.BlockSpec((1, d), lambda i, j: (0, 0)),
            pl.BlockSpec((d, tn), lambda i, j: (0, j)),
            pl.BlockSpec((PROJ_TILE, DIFF_DK), lambda i, j: (i, 0)),
            pl.BlockSpec((PROJ_TILE, DIFF_DK), lambda i, j: (i, 0)),
            pl.BlockSpec((None, 1, DIFF_DK), lambda i, j: (j // tiles_per_gain, 0, 0)),
        ],
        out_specs=pl.BlockSpec((tn // DIFF_DK, PROJ_TILE, DIFF_DK), lambda i, j: (j, i, 0)),
        out_shape=jax.ShapeDtypeStruct((n // DIFF_DK, m, DIFF_DK), BF16),
        scratch_shapes=[pltpu.VMEM((PROJ_TILE, d), BF16)],
        compiler_params=_params(("parallel", "arbitrary")),
        name="norm_matmul_qk",
    )(x, g.reshape(1, d), w, cos, sin, gains)


def _norm_matmul_t_kernel(x_ref, g_ref, wt_ref, o_ref, xn_ref):
    @pl.when(pl.program_id(1) == 0)
    def _():
        xn_ref[...] = _rms(x_ref[...], g_ref[...]).astype(BF16)

    o_ref[...] = lax.dot_general(wt_ref[...], xn_ref[...], (((1,), (1,)), ((), ())),
                                 preferred_element_type=F32).astype(o_ref.dtype)


def _norm_matmul_t(x, g, wt, tn):
    m, d = x.shape
    n = wt.shape[0]
    return pl.pallas_call(
        _norm_matmul_t_kernel,
        grid=(m // PROJ_TILE, n // tn),
        in_specs=[
            pl.BlockSpec((PROJ_TILE, d), lambda i, j: (i, 0)),
            pl.BlockSpec((1, d), lambda i, j: (0, 0)),
            pl.BlockSpec((tn, d), lambda i, j: (j, 0)),
        ],
        out_specs=pl.BlockSpec((tn, PROJ_TILE), lambda i, j: (j, i)),
        out_shape=jax.ShapeDtypeStruct((n, m), BF16),
        scratch_shapes=[pltpu.VMEM((PROJ_TILE, d), BF16)],
        compiler_params=_params(("parallel", "arbitrary")),
        name="norm_matmul_t",
    )(x, g.reshape(1, d), wt)


def _matmul_res_kernel(n_in, *refs):
    a_refs = refs[:n_in]
    w_ref, h_ref, keep_ref, o_ref = refs[n_in:]
    mix = None
    k0 = 0
    for a_ref in a_refs:
        k1 = k0 + a_ref.shape[1]
        part = jnp.dot(a_ref[...], w_ref[k0:k1, :], preferred_element_type=F32)
        mix = part if mix is None else mix + part
        k0 = k1
    o_ref[...] = h_ref[...] + mix * keep_ref[...]


def _matmul_res(a_parts, w, h, keep):
    m = h.shape[0]
    k, n = w.shape
    return pl.pallas_call(
        functools.partial(_matmul_res_kernel, len(a_parts)),
        grid=(m // ROW_TILE,),
        in_specs=[pl.BlockSpec((ROW_TILE, a.shape[1]), lambda i: (i, 0)) for a in a_parts] + [
            pl.BlockSpec((k, n), lambda i: (0, 0)),
            pl.BlockSpec((ROW_TILE, n), lambda i: (i, 0)),
            pl.BlockSpec((ROW_TILE, 1), lambda i: (i, 0)),
        ],
        out_specs=pl.BlockSpec((ROW_TILE, n), lambda i: (i, 0)),
        out_shape=jax.ShapeDtypeStruct((m, n), F32),
        compiler_params=_params(("parallel",)),
        name="matmul_res",
    )(*a_parts, w, h, keep)


def _gla_kernel(seq_valid_end, q_ref, k_ref, v_ref, g_ref, lr_ref, w2_ref, b_ref, nrm_ref, sel_ref,
                o_ref, st_ref, cum_ref, qs_ref, ks_ref):
    c = pl.program_id(1)

    @pl.when(c == 0)
    def _():
        st_ref[...] = jnp.zeros_like(st_ref)

    row = c * CHUNK + lax.broadcasted_iota(jnp.int32, (CHUNK, 1), 0)
    vm = jnp.where(row >= META_OFFSET, jnp.where(row < seq_valid_end, 1.0, 0.0), 0.0)
    ii = lax.broadcasted_iota(jnp.int32, (CHUNK, CHUNK), 0)
    jj = lax.broadcasted_iota(jnp.int32, (CHUNK, CHUNK), 1)
    tril = jnp.where(jj <= ii, 1.0, 0.0).astype(F32)

    gate = jnp.dot(lr_ref[...], w2_ref[...], preferred_element_type=F32) + b_ref[...]
    log_a = (jnp.minimum(gate, 0.0) - jnp.log(1.0 + jnp.exp(-jnp.abs(gate)))) * (1.0 / GLA_GATE_NORM) * vm

    for h in range(GLA_HEADS):
        ksl = slice(h * GLA_DK, (h + 1) * GLA_DK)
        cum_ref[h] = jnp.dot(tril, log_a[:, ksl], preferred_element_type=F32,
                             precision=lax.Precision.HIGHEST) * LOG2E
        qs_ref[h] = q_ref[:, ksl].astype(F32) * (GLA_DK ** -0.5)
        ks_ref[h] = k_ref[:, ksl].astype(F32) * vm

    def decay(cum, cj, r0):
        r1 = r0 + PAIR_BLOCK
        parts = [jnp.exp2(-jnp.abs(cum[r0:r1] - cj))]
        if r0 > 0:
            parts.insert(0, jnp.exp2(cj - cum[:r0]))
        if r1 < CHUNK:
            parts.append(jnp.exp2(cum[r1:] - cj))
        return jnp.concatenate(parts, axis=0)

    scores = []
    for h in range(GLA_HEADS):
        cum = cum_ref[h]
        q = qs_ref[h]
        total = None
        for jb in range(CHUNK // PAIR_BLOCK):
            ws = []
            for j in range(jb * PAIR_BLOCK, (jb + 1) * PAIR_BLOCK):
                cj = cum_ref[h, j:j + 1, :]
                kj = ks_ref[h, j:j + 1, :]
                ws.append((decay(cum, cj, jb * PAIR_BLOCK) * (q * kj)).astype(BF16))
            part = jnp.dot(jnp.concatenate(ws, axis=1), sel_ref[jb], preferred_element_type=F32)
            total = part if total is None else total + part
        scores.append(total)

    for h in range(GLA_HEADS):
        vsl = slice(h * GLA_DV, (h + 1) * GLA_DV)
        cum = cum_ref[h]
        v = v_ref[:, vsl]
        st = st_ref[h]
        q_in = (qs_ref[h] * jnp.exp2(cum)).astype(BF16)
        o = jnp.dot(scores[h].astype(BF16), v, preferred_element_type=F32)
        o = o + lax.dot_general(q_in, st.astype(BF16), (((1,), (1,)), ((), ())), preferred_element_type=F32)
        last = cum[CHUNK - 1:CHUNK, :]
        k_dec = (ks_ref[h] * jnp.exp2(last - cum)).astype(BF16)
        st_ref[h] = st * jnp.exp2(last) + lax.dot_general(v, k_dec, (((0,), (0,)), ((), ())),
                                                          preferred_element_type=F32)
        gv = g_ref[:, vsl].astype(F32)
        o_ref[:, vsl] = (_rms(o, nrm_ref[h]) * (gv * _sigmoid(gv))).astype(o_ref.dtype)


def _gla(proj, lr, w2, b_gate, nrm, sel, lpad, seq_valid_end):
    nb = proj.shape[0]
    hk = GLA_HEADS * GLA_DK
    hv = GLA_HEADS * GLA_DV
    return pl.pallas_call(
        functools.partial(_gla_kernel, seq_valid_end),
        grid=(nb, lpad // CHUNK),
        in_specs=[
            pl.BlockSpec((None, CHUNK, hk), lambda b, c: (b, c, 0)),
            pl.BlockSpec((None, CHUNK, hk), lambda b, c: (b, c, 1)),
            pl.BlockSpec((None, CHUNK, hv), lambda b, c: (b, c, 1)),
            pl.BlockSpec((None, CHUNK, hv), lambda b, c: (b, c, 2)),
            pl.BlockSpec((None, CHUNK, LANES), lambda b, c: (b, c, 0)),
            pl.BlockSpec((LANES, hk), lambda b, c: (0, 0)),
            pl.BlockSpec((1, hk), lambda b, c: (0, 0)),
            pl.BlockSpec((GLA_HEADS, 1, GLA_DV), lambda b, c: (0, 0, 0)),
            pl.BlockSpec((CHUNK // PAIR_BLOCK, PAIR_BLOCK * GLA_DK, CHUNK), lambda b, c: (0, 0, 0)),
        ],
        out_specs=pl.BlockSpec((None, CHUNK, hv), lambda b, c: (b, c, 0)),
        out_shape=jax.ShapeDtypeStruct((nb, lpad, hv), BF16),
        scratch_shapes=[
            pltpu.VMEM((GLA_HEADS, GLA_DV, GLA_DK), F32),
            pltpu.VMEM((GLA_HEADS, CHUNK, GLA_DK), F32),
            pltpu.VMEM((GLA_HEADS, CHUNK, GLA_DK), F32),
            pltpu.VMEM((GLA_HEADS, CHUNK, GLA_DK), F32),
        ],
        compiler_params=_params(("parallel", "arbitrary")),
        name="gla_scan",
    )(proj, proj, proj, proj, lr, w2, b_gate, nrm, sel)


def _ret_kernel(seq_valid_end, q_ref, k_ref, v_ref, g_ref, cos_ref, sin_ref, lg_ref, nrm_ref,
                o_ref, st_ref):
    c = pl.program_id(1)

    @pl.when(c == 0)
    def _():
        st_ref[...] = jnp.zeros_like(st_ref)

    row = c * CHUNK + lax.broadcasted_iota(jnp.int32, (CHUNK, 1), 0)
    vm = jnp.where(row >= META_OFFSET, jnp.where(row < seq_valid_end, 1.0, 0.0), 0.0)
    n_i = lax.broadcasted_iota(jnp.int32, (CHUNK, CHUNK), 0).astype(F32)
    n_j = lax.broadcasted_iota(jnp.int32, (CHUNK, CHUNK), 1).astype(F32)
    dist = jnp.abs(n_i - n_j)
    n_col = lax.broadcasted_iota(jnp.int32, (CHUNK, 1), 0).astype(F32)
    cos = cos_ref[...]
    sin = sin_ref[...]
    half = RET_DK // 2

    def rot(x):
        x1 = x[:, :half]
        x2 = x[:, half:]
        return jnp.concatenate([x1 * cos - x2 * sin, x2 * cos + x1 * sin], axis=-1)

    for h in range(RET_HEADS):
        ksl = slice(h * RET_DK, (h + 1) * RET_DK)
        vsl = slice(h * RET_DV, (h + 1) * RET_DV)
        lg = lg_ref[h]
        intra = jnp.exp(dist * lg)
        xi = jnp.exp((n_col + 1.0) * lg)
        zeta = jnp.exp((CHUNK - 1.0 - n_col) * lg)
        g_chunk = jnp.exp(CHUNK * lg)
        qr = rot(q_ref[:, ksl].astype(F32)).astype(BF16)
        kr = rot(k_ref[:, ksl].astype(F32)) * (RET_DK ** -0.5) * vm
        v = v_ref[:, vsl]
        st = st_ref[h]
        s = lax.dot_general(qr, kr.astype(BF16), (((1,), (1,)), ((), ())), preferred_element_type=F32) * intra
        o = jnp.dot(s.astype(BF16), v, preferred_element_type=F32)
        o = o + lax.dot_general(qr, st.astype(BF16), (((1,), (1,)), ((), ())), preferred_element_type=F32) * xi
        st_ref[h] = st * g_chunk + lax.dot_general(v, (kr * zeta).astype(BF16), (((0,), (0,)), ((), ())),
                                                   preferred_element_type=F32)
        gv = g_ref[:, vsl].astype(F32)
        o_ref[:, vsl] = (_rms(o, nrm_ref[h]) * (gv * _sigmoid(gv))).astype(o_ref.dtype)


def _ret(proj, cos, sin, lg, nrm, lpad, seq_valid_end):
    nb = proj.shape[0]
    hk = RET_HEADS * RET_DK
    hv = RET_HEADS * RET_DV
    base = (2 * GLA_HEADS * GLA_DK + 2 * GLA_HEADS * GLA_DV) // hk
    return pl.pallas_call(
        functools.partial(_ret_kernel, seq_valid_end),
        grid=(nb, lpad // CHUNK),
        in_specs=[
            pl.BlockSpec((None, CHUNK, hk), lambda b, c: (b, c, base)),
            pl.BlockSpec((None, CHUNK, hk), lambda b, c: (b, c, base + 1)),
            pl.BlockSpec((None, CHUNK, hv), lambda b, c: (b, c, base + 2)),
            pl.BlockSpec((None, CHUNK, hv), lambda b, c: (b, c, base + 3)),
            pl.BlockSpec((CHUNK, RET_DK // 2), lambda b, c: (c, 0)),
            pl.BlockSpec((CHUNK, RET_DK // 2), lambda b, c: (c, 0)),
            pl.BlockSpec((RET_HEADS, 1, 1), lambda b, c: (0, 0, 0)),
            pl.BlockSpec((RET_HEADS, 1, RET_DV), lambda b, c: (0, 0, 0)),
        ],
        out_specs=pl.BlockSpec((None, CHUNK, hv), lambda b, c: (b, c, 0)),
        out_shape=jax.ShapeDtypeStruct((nb, lpad, hv), BF16),
        scratch_shapes=[pltpu.VMEM((RET_HEADS, RET_DV, RET_DK), F32)],
        compiler_params=_params(("parallel", "arbitrary")),
        name="ret_scan",
    )(proj, proj, proj, proj, cos, sin, lg, nrm)


def _diff_attn_kernel(seq_valid_end, lam_init, lpad, q1_ref, q2_ref, k1_ref, k2_ref, vt_ref, lam_ref, nrm_ref,
                      o_ref, acc1_ref, acc2_ref):
    qi = pl.program_id(2)
    t = ATTN_TILE
    tq = ATTN_Q_TILE
    qs = (q1_ref[...], q2_ref[...])
    k_refs = (k1_ref, k2_ref)
    acc_refs = (acc1_ref, acc2_ref)
    q_chunk = lax.shift_right_logical(qi * tq + lax.broadcasted_iota(jnp.int32, (1, tq), 1), 6)
    nt_dims = (((1,), (1,)), ((), ()))

    def start(j):
        return j * t if isinstance(j, int) else pl.multiple_of(j * t, t)

    def scores(j, masked):
        out = []
        if masked:
            k_idx = j * t + lax.broadcasted_iota(jnp.int32, (t, 1), 0)
            k_chunk = jnp.where(k_idx >= META_OFFSET,
                                jnp.where(k_idx < seq_valid_end, lax.shift_right_logical(k_idx, 6), 2**30), 2**30)
            mask = k_chunk <= q_chunk
        for b in range(2):
            s = lax.dot_general(k_refs[b][pl.ds(start(j), t), :], qs[b], nt_dims, preferred_element_type=F32)
            out.append(jnp.where(mask, s, -1e30) if masked else s)
        return tuple(out)

    def update(j, stats, s_cur):
        vt = vt_ref[:, pl.ds(start(j), t)]
        new = []
        for b in range(2):
            m, l = stats[b]
            s = s_cur[b]
            m_new = jnp.maximum(m, jnp.max(s, axis=0, keepdims=True))
            alpha = jnp.exp2(m - m_new)
            p = jnp.exp2(s - m_new)
            l_new = alpha * l + jnp.sum(p, axis=0, keepdims=True)
            acc_refs[b][...] = alpha * acc_refs[b][...] + jnp.dot(vt, p.astype(BF16), preferred_element_type=F32)
            new.append((m_new, l_new))
        return tuple(new)

    def masked_body(j, carry):
        stats, s_cur = carry
        s_next = scores(j + 1, True)
        return update(j, stats, s_cur), s_next

    def pair_body(p, carry):
        stats, s_cur = carry
        j = 2 * p
        s_mid = scores(j + 1, False)
        stats = update(j, stats, s_cur)
        s_next = scores(j + 2, False)
        return update(j + 1, stats, s_mid), s_next

    for b in range(2):
        acc_refs[b][...] = jnp.zeros_like(acc_refs[b])
    init = (jnp.full((1, tq), -1e30, F32), jnp.zeros((1, tq), F32))
    carry = ((init, init), scores(0, True))
    first_diag = qi * (tq // t)
    n_tiles = jnp.minimum(first_diag + tq // t, lpad // t)
    n_plain = jnp.maximum(first_diag - 1, 0)
    n_pairs = lax.shift_right_logical(n_plain, 1)
    carry = lax.fori_loop(0, n_pairs, pair_body, carry)
    carry = lax.fori_loop(2 * n_pairs, n_tiles - 1, masked_body, carry)
    (_, l1), (_, l2) = update(n_tiles - 1, *carry)

    lam_v = lam_ref[...]
    lam = (jnp.exp(jnp.sum(lam_v[0:1] * lam_v[1:2], axis=-1, keepdims=True))
           - jnp.exp(jnp.sum(lam_v[2:3] * lam_v[3:4], axis=-1, keepdims=True)) + lam_init)
    o_t = acc1_ref[...] * (1.0 / l1) - acc2_ref[...] * (lam * (1.0 / l2))
    o_ref[...] = (_rms(o_t.T, nrm_ref[...]) * (1.0 - lam_init)).astype(o_ref.dtype)


def _diff_attn(qk, vt, lam_vecs, nrm, lpad, seq_valid_end, lam_init):
    nb = qk.shape[1]
    t = ATTN_TILE
    tq = ATTN_Q_TILE
    kb = 2 * DIFF_HEADS
    return pl.pallas_call(
        functools.partial(_diff_attn_kernel, seq_valid_end, lam_init, lpad),
        grid=(nb, DIFF_HEADS, pl.cdiv(lpad, tq)),
        in_specs=[
            pl.BlockSpec((None, None, tq, DIFF_DK), lambda b, h, i: (2 * h, b, i, 0)),
            pl.BlockSpec((None, None, tq, DIFF_DK), lambda b, h, i: (2 * h + 1, b, i, 0)),
            pl.BlockSpec((None, None, lpad, DIFF_DK), lambda b, h, i: (kb + 2 * h, b, 0, 0)),
            pl.BlockSpec((None, None, lpad, DIFF_DK), lambda b, h, i: (kb + 2 * h + 1, b, 0, 0)),
            pl.BlockSpec((DIFF_DV, lpad), lambda b, h, i: (h, b)),
            pl.BlockSpec((4, DIFF_DK), lambda b, h, i: (0, 0)),
            pl.BlockSpec((1, DIFF_DV), lambda b, h, i: (0, 0)),
        ],
        out_specs=pl.BlockSpec((None, tq, DIFF_DV), lambda b, h, i: (b, i, h)),
        out_shape=jax.ShapeDtypeStruct((nb, lpad, DIFF_HEADS * DIFF_DV), BF16),
        scratch_shapes=[pltpu.VMEM((DIFF_DV, tq), F32), pltpu.VMEM((DIFF_DV, tq), F32)],
        compiler_params=_params(("parallel", "parallel", "arbitrary")),
        name="diff_attn",
    )(qk, qk, qk, qk, vt, lam_vecs, nrm)


def _ffn_kernel(nf, x_ref, halo_ref, g_ref, wv_ref, wg_ref, cwv_ref, cwg_ref, cbv_ref, cbg_ref, wd0_ref, wd1_ref,
                keep_ref, o_ref, hn_ref, acc_ref, act1a_ref, act1b_ref, uv0_ref, ug0_ref, uv1_ref, ug1_ref):
    i = pl.program_id(0)
    j = pl.program_id(1)
    tm = x_ref.shape[0]
    sl0 = slice(0, FF_SUB)
    sl1 = slice(FF_SUB, FF_TILE)
    staged = (act1a_ref, act1b_ref)

    @pl.when(j == 0)
    def _():
        g = g_ref[...]
        hn_ref[HALO:, :] = _rms(x_ref[...], g).astype(BF16)
        first = jnp.where(i > 0, 1.0, 0.0)
        hn_ref[:HALO, :] = (_rms(halo_ref[...], g) * first).astype(BF16)
        acc_ref[...] = jnp.zeros_like(acc_ref)

    def up():
        hn = hn_ref[...]
        uv0_ref[...] = jnp.dot(hn, wv_ref[:, sl0], preferred_element_type=F32)
        ug0_ref[...] = jnp.dot(hn, wg_ref[:, sl0], preferred_element_type=F32)
        uv1_ref[...] = jnp.dot(hn, wv_ref[:, sl1], preferred_element_type=F32)
        ug1_ref[...] = jnp.dot(hn, wg_ref[:, sl1], preferred_element_type=F32)

    def conv(u_ref, cw_ref, cb_ref, sl):
        cw = cw_ref[:, sl]
        c = cb_ref[:, sl] + cw[0:1] * u_ref[pl.ds(HALO - 2, tm), :]
        c = c + cw[1:2] * u_ref[pl.ds(HALO - 1, tm), :]
        return c + cw[2:3] * u_ref[pl.ds(HALO, tm), :]

    def activation(uv_ref, ug_ref, sl):
        gate = conv(ug_ref, cwg_ref, cbg_ref, sl)
        return (conv(uv_ref, cwv_ref, cbv_ref, sl) * (gate * _sigmoid(gate))).astype(BF16)

    def down_previous_second(src_ref):
        return jnp.dot(src_ref[...], wd1_ref[...], preferred_element_type=F32)

    def down_first_stage_second(pending, dst_ref):
        part = jnp.dot(activation(uv0_ref, ug0_ref, sl0), wd0_ref[...], preferred_element_type=F32)
        acc_ref[...] += part if pending is None else pending + part
        dst_ref[...] = activation(uv1_ref, ug1_ref, sl1)

    @pl.when(j == 0)
    def _():
        up()
        down_first_stage_second(None, staged[0])

    for parity in range(2):
        @pl.when(jnp.logical_and(jnp.logical_and(j > 0, j < nf), j % 2 == parity))
        def _():
            up()
            down_first_stage_second(down_previous_second(staged[1 - parity]), staged[parity])

    @pl.when(j == nf)
    def _():
        o_ref[...] = x_ref[...] + (acc_ref[...] + down_previous_second(staged[(nf - 1) % 2])) * keep_ref[...]


def _ffn(x, g, w_up, conv_w, conv_b, w_down, keep):
    m, d = x.shape
    f = w_down.shape[0]
    nf = f // FF_TILE
    tm = ROW_TILE
    halo_blocks = tm // HALO
    assert FF_TILE == 2 * FF_SUB

    def cur(j):
        return jnp.minimum(j, nf - 1)

    u_buf = pltpu.VMEM((tm + HALO, FF_SUB), F32)
    act_buf = pltpu.VMEM((tm, FF_SUB), BF16)
    return pl.pallas_call(
        functools.partial(_ffn_kernel, nf),
        grid=(m // tm, nf + 1),
        in_specs=[
            pl.BlockSpec((tm, d), lambda i, j: (i, 0)),
            pl.BlockSpec((None, HALO, d), lambda i, j: (jnp.maximum(i * halo_blocks - 1, 0), 0, 0)),
            pl.BlockSpec((1, d), lambda i, j: (0, 0)),
            pl.BlockSpec((d, FF_TILE), lambda i, j: (0, cur(j))),
            pl.BlockSpec((d, FF_TILE), lambda i, j: (0, nf + cur(j))),
            pl.BlockSpec((CONV_W, FF_TILE), lambda i, j: (0, cur(j))),
            pl.BlockSpec((CONV_W, FF_TILE), lambda i, j: (0, nf + cur(j))),
            pl.BlockSpec((1, FF_TILE), lambda i, j: (0, cur(j))),
            pl.BlockSpec((1, FF_TILE), lambda i, j: (0, nf + cur(j))),
            pl.BlockSpec((FF_SUB, d), lambda i, j: (2 * cur(j), 0)),
            pl.BlockSpec((FF_SUB, d), lambda i, j: (2 * jnp.maximum(j - 1, 0) + 1, 0)),
            pl.BlockSpec((tm, 1), lambda i, j: (i, 0)),
        ],
        out_specs=pl.BlockSpec((tm, d), lambda i, j: (i, 0)),
        out_shape=jax.ShapeDtypeStruct((m, d), F32),
        scratch_shapes=[pltpu.VMEM((tm + HALO, d), BF16), pltpu.VMEM((tm, d), F32),
                        act_buf, act_buf, u_buf, u_buf, u_buf, u_buf],
        compiler_params=_params(("parallel", "arbitrary")),
        name="conv_ffn",
    )(x, x.reshape(m // HALO, HALO, d), g.reshape(1, d), w_up, w_up, conv_w, conv_w,
      conv_b.reshape(1, -1), conv_b.reshape(1, -1), w_down, w_down, keep)


def _rope_tables(pos, inv_freq):
    ang = pos.astype(F32)[:, None] * inv_freq[None, :]
    return jnp.cos(ang), jnp.sin(ang)


def kernel(x, meta, mix_norm_e, w_in_e, gla_w_gate_e, gla_b_gate_e, gla_norm_e, ret_norm_e, w_out_e, mix_norm_o, w_in_o, q_norm_o, k_norm_o, lam_q1_o, lam_k1_o, lam_q2_o, lam_k2_o, diff_norm_o, w_out_o, ffn_norm, w_up, conv_w, conv_b, w_down):
    nb, s, d = x.shape
    seq_valid_end = CHUNK + s
    lpad = -(-seq_valid_end // ATTN_TILE) * ATTN_TILE
    m = nb * lpad

    h = jnp.concatenate([jnp.zeros((nb, META_OFFSET, d), x.dtype),
                         jnp.broadcast_to(meta.astype(x.dtype)[None], (nb, N_META, d)),
                         x,
                         jnp.zeros((nb, lpad - seq_valid_end, d), x.dtype)], axis=1).reshape(m, d)
    idx = jnp.arange(lpad)
    valid = (idx >= META_OFFSET) & (idx < seq_valid_end)
    pos = idx - META_OFFSET
    keep = jnp.tile(valid.astype(F32), nb).reshape(m, 1)

    gla_cols = 2 * GLA_HEADS * GLA_DK + 2 * GLA_HEADS * GLA_DV
    w_in = w_in_e[0]
    w_main = jnp.concatenate([w_in[:, :gla_cols], w_in[:, gla_cols + GLA_GATE_RANK:]], axis=1).astype(BF16)
    w_lr = jnp.pad(w_in[:, gla_cols:gla_cols + GLA_GATE_RANK], ((0, 0), (0, LANES - GLA_GATE_RANK))).astype(BF16)
    w_gate2 = jnp.pad(gla_w_gate_e[0], ((0, LANES - GLA_GATE_RANK), (0, 0))).astype(BF16)
    proj = _norm_matmul(h, mix_norm_e[0], w_main, 3584).reshape(nb, lpad, -1)
    lr = _norm_matmul(h, mix_norm_e[0], w_lr, LANES).reshape(nb, lpad, LANES)
    sel = (jnp.arange(CHUNK)[:, None, None] == jnp.arange(CHUNK)[None, None, :]).astype(BF16)
    sel = jnp.broadcast_to(sel, (CHUNK, GLA_DK, CHUNK)).reshape(CHUNK // PAIR_BLOCK, PAIR_BLOCK * GLA_DK, CHUNK)
    o_a = _gla(proj, lr, w_gate2, gla_b_gate_e[0].reshape(1, -1), gla_norm_e[0].reshape(GLA_HEADS, 1, GLA_DV),
               sel, lpad, seq_valid_end)
    inv_freq_ret = 1.0 / (ROPE_THETA ** jnp.linspace(0.0, 1.0, RET_DK // 2, dtype=F32))
    cos_r, sin_r = _rope_tables(pos, inv_freq_ret)
    log_gamma = jnp.log(1.0 - 2.0 ** (-5.0 - jnp.arange(RET_HEADS, dtype=F32))).reshape(RET_HEADS, 1, 1)
    o_b = _ret(proj, cos_r, sin_r, log_gamma, ret_norm_e[0].reshape(RET_HEADS, 1, RET_DV), lpad, seq_valid_end)
    h = _matmul_res([o_a.reshape(m, -1), o_b.reshape(m, -1)], w_out_e[0].astype(BF16), h, keep)
    h = _ffn(h, ffn_norm[0], w_up[0].astype(BF16), conv_w[0], conv_b[0], w_down[0].astype(BF16), keep)

    layer = 1
    lam_init = 0.8 - 0.6 * math.exp(-0.3 * layer)
    qk_cols = 4 * DIFF_HEADS * DIFF_DK
    inv_freq = 1.0 / (ROPE_THETA ** (jnp.arange(0, DIFF_DK, 2, dtype=F32) / DIFF_DK))
    cos_d, sin_d = _rope_tables(pos, inv_freq)
    cos_full = jnp.tile(jnp.concatenate([cos_d, cos_d], axis=-1), (nb, 1))
    sin_signed = jnp.tile(jnp.concatenate([-sin_d, sin_d], axis=-1), (nb, 1))
    gains = jnp.stack([q_norm_o[0] * (DIFF_DK ** -0.5 * LOG2E), k_norm_o[0]]).reshape(2, 1, DIFF_DK)
    qk = _norm_matmul_qk(h, mix_norm_o[0], w_in_o[0][:, :qk_cols].astype(BF16), cos_full, sin_signed, gains,
                         2048).reshape(qk_cols // DIFF_DK, nb, lpad, DIFF_DK)
    vt = _norm_matmul_t(h, mix_norm_o[0], w_in_o[0][:, qk_cols:].T.astype(BF16), 2048)
    lam_vecs = jnp.stack([lam_q1_o[0], lam_k1_o[0], lam_q2_o[0], lam_k2_o[0]]).astype(F32)
    o = _diff_attn(qk, vt, lam_vecs, diff_norm_o[0].reshape(1, -1), lpad, seq_valid_end, lam_init)
    h = _matmul_res([o.reshape(m, -1)], w_out_o[0].astype(BF16), h, keep)
    h = _ffn(h, ffn_norm[1], w_up[1].astype(BF16), conv_w[1], conv_b[1], w_down[1].astype(BF16), keep)

    return h.reshape(nb, lpad, d)[:, CHUNK:CHUNK + s]
```

```python
import functools
import math

import jax
import jax.numpy as jnp
from jax import lax
from jax.experimental import pallas as pl
from jax.experimental.pallas import tpu as pltpu

F32 = jnp.float32
BF16 = jnp.bfloat16

D_MODEL = 2048
CHUNK = 64
N_META = 16
META_OFFSET = CHUNK - N_META
EPS = 1e-6
ROPE_THETA = 10000.0

GLA_HEADS = 4
GLA_DK = 128
GLA_DV = 256
GLA_GATE_RANK = 16
GLA_GATE_NORM = 16.0

RET_HEADS = 4
RET_DK = 256
RET_DV = 256

DIFF_HEADS = 8
DIFF_DK = 128
DIFF_DV = 256

D_FF = 5632
CONV_W = 3

LANES = 128
ATTN_TILE = 256
ATTN_Q_TILE = 512
ROW_TILE = 512
PROJ_TILE = 512
HALO = 16
FF_TILE = 512
FF_SUB = 256
VMEM_LIMIT = 56 * 2**20
PAIR_BLOCK = 8
LOG2E = math.log2(math.e)


def _params(semantics):
    return pltpu.CompilerParams(dimension_semantics=semantics, vmem_limit_bytes=VMEM_LIMIT)


def _rms(x, g):
    return x * lax.rsqrt(jnp.mean(x * x, axis=-1, keepdims=True) + EPS) * g


def _sigmoid(x):
    return 1.0 / (1.0 + jnp.exp(-x))


def _norm_matmul_kernel(x_ref, g_ref, w_ref, w_side_ref, o_ref, o_side_ref, xn_ref):
    @pl.when(pl.program_id(1) == 0)
    def _():
        xn_ref[...] = _rms(x_ref[...], g_ref[...]).astype(BF16)
        o_side_ref[...] = jnp.dot(xn_ref[...], w_side_ref[...], preferred_element_type=F32).astype(o_side_ref.dtype)

    o_ref[...] = jnp.dot(xn_ref[...], w_ref[...], preferred_element_type=F32).astype(o_ref.dtype)


def _norm_matmul(x, g, w, w_side, tn):
    m, d = x.shape
    n = w.shape[1]
    n_side = w_side.shape[1]
    return pl.pallas_call(
        _norm_matmul_kernel,
        grid=(m // PROJ_TILE, n // tn),
        in_specs=[
            pl.BlockSpec((PROJ_TILE, d), lambda i, j: (i, 0)),
            pl.BlockSpec((1, d), lambda i, j: (0, 0)),
            pl.BlockSpec((d, tn), lambda i, j: (0, j)),
            pl.BlockSpec((d, n_side), lambda i, j: (0, 0)),
        ],
        out_specs=[pl.BlockSpec((PROJ_TILE, tn), lambda i, j: (i, j)),
                   pl.BlockSpec((PROJ_TILE, n_side), lambda i, j: (i, 0))],
        out_shape=[jax.ShapeDtypeStruct((m, n), BF16), jax.ShapeDtypeStruct((m, n_side), BF16)],
        scratch_shapes=[pltpu.VMEM((PROJ_TILE, d), BF16)],
        compiler_params=_params(("parallel", "arbitrary")),
        name="norm_matmul",
    )(x, g.reshape(1, d), w, w_side)


def _norm_matmul_qk_kernel(x_ref, g_ref, w_ref, cos_ref, sin_ref, gain_ref, o_ref, xn_ref):
    @pl.when(pl.program_id(1) == 0)
    def _():
        xn_ref[...] = _rms(x_ref[...], g_ref[...]).astype(BF16)

    y = jnp.dot(xn_ref[...], w_ref[...], preferred_element_type=F32)
    cos = cos_ref[...]
    sin = sin_ref[...]
    gain = gain_ref[...]
    for s in range(y.shape[1] // DIFF_DK):
        sl = slice(s * DIFF_DK, (s + 1) * DIFF_DK)
        ys = _rms(y[:, sl], gain)
        o_ref[s] = (ys * cos + pltpu.roll(ys, DIFF_DK // 2, axis=1) * sin).astype(o_ref.dtype)


def _norm_matmul_qk(x, g, w, cos, sin, gains, tn):
    m, d = x.shape
    n = w.shape[1]
    tiles_per_gain = n // (2 * tn)
    return pl.pallas_call(
        _norm_matmul_qk_kernel,
        grid=(m // PROJ_TILE, n // tn),
        in_specs=[
            pl.BlockSpec((PROJ_TILE, d), lambda i, j: (i, 0)),
            pl.BlockSpec((1, d), lambda i, j: (0, 0)),
            pl.BlockSpec((d, tn), lambda i, j: (0, j)),
            pl.BlockSpec((PROJ_TILE, DIFF_DK), lambda i, j: (i, 0)),
            pl.BlockSpec((PROJ_TILE, DIFF_DK), lambda i, j: (i, 0)),
            pl.BlockSpec((None, 1, DIFF_DK), lambda i, j: (j // tiles_per_gain, 0, 0)),
        ],
        out_specs=pl.BlockSpec((tn // DIFF_DK, PROJ_TILE, DIFF_DK), lambda i, j: (j, i, 0)),
        out_shape=jax.ShapeDtypeStruct((n // DIFF_DK, m, DIFF_DK), BF16),
        scratch_shapes=[pltpu.VMEM((PROJ_TILE, d), BF16)],
        compiler_params=_params(("parallel", "arbitrary")),
        name="norm_matmul_qk",
    )(x, g.reshape(1, d), w, cos, sin, gains)


def _norm_matmul_t_kernel(x_ref, g_ref, wt_ref, o_ref, xn_ref):
    @pl.when(pl.program_id(1) == 0)
    def _():
        xn_ref[...] = _rms(x_ref[...], g_ref[...]).astype(BF16)

    o_ref[...] = lax.dot_general(wt_ref[...], xn_ref[...], (((1,), (1,)), ((), ())),
                                 preferred_element_type=F32).astype(o_ref.dtype)


def _norm_matmul_t(x, g, wt, tn):
    m, d = x.shape
    n = wt.shape[0]
    return pl.pallas_call(
        _norm_matmul_t_kernel,
        grid=(m // PROJ_TILE, n // tn),
        in_specs=[
            pl.BlockSpec((PROJ_TILE, d), lambda i, j: (i, 0)),
            pl.BlockSpec((1, d), lambda i, j: (0, 0)),
            pl.BlockSpec((tn, d), lambda i, j: (j, 0)),
        ],
        out_specs=pl.BlockSpec((tn, PROJ_TILE), lambda i, j: (j, i)),
        out_shape=jax.ShapeDtypeStruct((n, m), BF16),
        scratch_shapes=[pltpu.VMEM((PROJ_TILE, d), BF16)],
        compiler_params=_params(("parallel", "arbitrary")),
        name="norm_matmul_t",
    )(x, g.reshape(1, d), wt)


def _matmul_res_kernel(n_in, *refs):
    a_refs = refs[:n_in]
    w_ref, h_ref, keep_ref, o_ref = refs[n_in:]
    mix = None
    k0 = 0
    for a_ref in a_refs:
        k1 = k0 + a_ref.shape[1]
        part = jnp.dot(a_ref[...], w_ref[k0:k1, :], preferred_element_type=F32)
        mix = part if mix is None else mix + part
        k0 = k1
    o_ref[...] = h_ref[...] + mix * keep_ref[...]


def _matmul_res(a_parts, w, h, keep):
    m = h.shape[0]
    k, n = w.shape
    return pl.pallas_call(
        functools.partial(_matmul_res_kernel, len(a_parts)),
        grid=(m // ROW_TILE,),
        in_specs=[pl.BlockSpec((ROW_TILE, a.shape[1]), lambda i: (i, 0)) for a in a_parts] + [
            pl.BlockSpec((k, n), lambda i: (0, 0)),
            pl.BlockSpec((ROW_TILE, n), lambda i: (i, 0)),
            pl.BlockSpec((ROW_TILE, 1), lambda i: (i, 0)),
        ],
        out_specs=pl.BlockSpec((ROW_TILE, n), lambda i: (i, 0)),
        out_shape=jax.ShapeDtypeStruct((m, n), F32),
        compiler_params=_params(("parallel",)),
        name="matmul_res",
    )(*a_parts, w, h, keep)


def _gla_kernel(seq_valid_end, q_ref, k_ref, v_ref, g_ref, lr_ref, w2_ref, b_ref, nrm_ref, sel_ref,
                o_ref, st_ref, cum_ref, qs_ref, ks_ref):
    c = pl.program_id(1)

    @pl.when(c == 0)
    def _():
        st_ref[...] = jnp.zeros_like(st_ref)

    row = c * CHUNK + lax.broadcasted_iota(jnp.int32, (CHUNK, 1), 0)
    vm = jnp.where(row >= META_OFFSET, jnp.where(row < seq_valid_end, 1.0, 0.0), 0.0)
    ii = lax.broadcasted_iota(jnp.int32, (CHUNK, CHUNK), 0)
    jj = lax.broadcasted_iota(jnp.int32, (CHUNK, CHUNK), 1)
    tril = jnp.where(jj <= ii, 1.0, 0.0).astype(F32)

    gate = jnp.dot(lr_ref[...], w2_ref[...], preferred_element_type=F32) + b_ref[...]
    log_a = (jnp.minimum(gate, 0.0) - jnp.log(1.0 + jnp.exp(-jnp.abs(gate)))) * (1.0 / GLA_GATE_NORM) * vm

    for h in range(GLA_HEADS):
        ksl = slice(h * GLA_DK, (h + 1) * GLA_DK)
        cum_ref[h] = jnp.dot(tril, log_a[:, ksl], preferred_element_type=F32,
                             precision=lax.Precision.HIGHEST) * LOG2E
        qs_ref[h] = q_ref[:, ksl].astype(F32) * (GLA_DK ** -0.5)
        ks_ref[h] = k_ref[:, ksl].astype(F32) * vm

    def decay(cum, cj, r0):
        r1 = r0 + PAIR_BLOCK
        parts = [jnp.exp2(-jnp.abs(cum[r0:r1] - cj))]
        if r0 > 0:
            parts.insert(0, jnp.exp2(cj - cum[:r0]))
        if r1 < CHUNK:
            parts.append(jnp.exp2(cum[r1:] - cj))
        return jnp.concatenate(parts, axis=0)

    scores = []
    for h in range(GLA_HEADS):
        cum = cum_ref[h]
        q = qs_ref[h]
        total = None
        for jb in range(CHUNK // PAIR_BLOCK):
            ws = []
            for j in range(jb * PAIR_BLOCK, (jb + 1) * PAIR_BLOCK):
                cj = cum_ref[h, j:j + 1, :]
                kj = ks_ref[h, j:j + 1, :]
                ws.append((decay(cum, cj, jb * PAIR_BLOCK) * (q * kj)).astype(BF16))
            part = jnp.dot(jnp.concatenate(ws, axis=1), sel_ref[jb], preferred_element_type=F32)
            total = part if total is None else total + part
        scores.append(total)

    for h in range(GLA_HEADS):
        vsl = slice(h * GLA_DV, (h + 1) * GLA_DV)
        cum = cum_ref[h]
        v = v_ref[:, vsl]
        st = st_ref[h]
        q_in = (qs_ref[h] * jnp.exp2(cum)).astype(BF16)
        o = jnp.dot(scores[h].astype(BF16), v, preferred_element_type=F32)
        o = o + lax.dot_general(q_in, st.astype(BF16), (((1,), (1,)), ((), ())), preferred_element_type=F32)
        last = cum[CHUNK - 1:CHUNK, :]
        k_dec = (ks_ref[h] * jnp.exp2(last - cum)).astype(BF16)
        st_ref[h] = st * jnp.exp2(last) + lax.dot_general(v, k_dec, (((0,), (0,)), ((), ())),
                                                          preferred_element_type=F32)
        gv = g_ref[:, vsl].astype(F32)
        o_ref[:, vsl] = (_rms(o, nrm_ref[h]) * (gv * _sigmoid(gv))).astype(o_ref.dtype)


def _gla(proj, lr, w2, b_gate, nrm, sel, lpad, seq_valid_end):
    nb = proj.shape[0]
    hk = GLA_HEADS * GLA_DK
    hv = GLA_HEADS * GLA_DV
    return pl.pallas_call(
        functools.partial(_gla_kernel, seq_valid_end),
        grid=(nb, lpad // CHUNK),
        in_specs=[
            pl.BlockSpec((None, CHUNK, hk), lambda b, c: (b, c, 0)),
            pl.BlockSpec((None, CHUNK, hk), lambda b, c: (b, c, 1)),
            pl.BlockSpec((None, CHUNK, hv), lambda b, c: (b, c, 1)),
            pl.BlockSpec((None, CHUNK, hv), lambda b, c: (b, c, 2)),
            pl.BlockSpec((None, CHUNK, LANES), lambda b, c: (b, c, 0)),
            pl.BlockSpec((LANES, hk), lambda b, c: (0, 0)),
            pl.BlockSpec((1, hk), lambda b, c: (0, 0)),
            pl.BlockSpec((GLA_HEADS, 1, GLA_DV), lambda b, c: (0, 0, 0)),
            pl.BlockSpec((CHUNK // PAIR_BLOCK, PAIR_BLOCK * GLA_DK, CHUNK), lambda b, c: (0, 0, 0)),
        ],
        out_specs=pl.BlockSpec((None, CHUNK, hv), lambda b, c: (b, c, 0)),
        out_shape=jax.ShapeDtypeStruct((nb, lpad, hv), BF16),
        scratch_shapes=[
            pltpu.VMEM((GLA_HEADS, GLA_DV, GLA_DK), F32),
            pltpu.VMEM((GLA_HEADS, CHUNK, GLA_DK), F32),
            pltpu.VMEM((GLA_HEADS, CHUNK, GLA_DK), F32),
            pltpu.VMEM((GLA_HEADS, CHUNK, GLA_DK), F32),
        ],
        compiler_params=_params(("parallel", "arbitrary")),
        name="gla_scan",
    )(proj, proj, proj, proj, lr, w2, b_gate, nrm, sel)


def _ret_kernel(seq_valid_end, q_ref, k_ref, v_ref, g_ref, cos_ref, sin_ref, lg_ref, nrm_ref,
                o_ref, st_ref):
    c = pl.program_id(1)

    @pl.when(c == 0)
    def _():
        st_ref[...] = jnp.zeros_like(st_ref)

    row = c * CHUNK + lax.broadcasted_iota(jnp.int32, (CHUNK, 1), 0)
    vm = jnp.where(row >= META_OFFSET, jnp.where(row < seq_valid_end, 1.0, 0.0), 0.0)
    n_i = lax.broadcasted_iota(jnp.int32, (CHUNK, CHUNK), 0).astype(F32)
    n_j = lax.broadcasted_iota(jnp.int32, (CHUNK, CHUNK), 1).astype(F32)
    dist = jnp.abs(n_i - n_j)
    n_col = lax.broadcasted_iota(jnp.int32, (CHUNK, 1), 0).astype(F32)
    cos = cos_ref[...]
    sin = sin_ref[...]
    half = RET_DK // 2

    def rot(x):
        x1 = x[:, :half]
        x2 = x[:, half:]
        return jnp.concatenate([x1 * cos - x2 * sin, x2 * cos + x1 * sin], axis=-1)

    for h in range(RET_HEADS):
        ksl = slice(h * RET_DK, (h + 1) * RET_DK)
        vsl = slice(h * RET_DV, (h + 1) * RET_DV)
        lg = lg_ref[h]
        intra = jnp.exp(dist * lg)
        xi = jnp.exp((n_col + 1.0) * lg)
        zeta = jnp.exp((CHUNK - 1.0 - n_col) * lg)
        g_chunk = jnp.exp(CHUNK * lg)
        qr = rot(q_ref[:, ksl].astype(F32)).astype(BF16)
        kr = rot(k_ref[:, ksl].astype(F32)) * (RET_DK ** -0.5) * vm
        v = v_ref[:, vsl]
        st = st_ref[h]
        s = lax.dot_general(qr, kr.astype(BF16), (((1,), (1,)), ((), ())), preferred_element_type=F32) * intra
        o = jnp.dot(s.astype(BF16), v, preferred_element_type=F32)
        o = o + lax.dot_general(qr, st.astype(BF16), (((1,), (1,)), ((), ())), preferred_element_type=F32) * xi
        st_ref[h] = st * g_chunk + lax.dot_general(v, (kr * zeta).astype(BF16), (((0,), (0,)), ((), ())),
                                                   preferred_element_type=F32)
        gv = g_ref[:, vsl].astype(F32)
        o_ref[:, vsl] = (_rms(o, nrm_ref[h]) * (gv * _sigmoid(gv))).astype(o_ref.dtype)


def _ret(proj, cos, sin, lg, nrm, lpad, seq_valid_end):
    nb = proj.shape[0]
    hk = RET_HEADS * RET_DK
    hv = RET_HEADS * RET_DV
    base = (2 * GLA_HEADS * GLA_DK + 2 * GLA_HEADS * GLA_DV) // hk
    return pl.pallas_call(
        functools.partial(_ret_kernel, seq_valid_end),
        grid=(nb, lpad // CHUNK),
        in_specs=[
            pl.BlockSpec((None, CHUNK, hk), lambda b, c: (b, c, base)),
            pl.BlockSpec((None, CHUNK, hk), lambda b, c: (b, c, base + 1)),
            pl.BlockSpec((None, CHUNK, hv), lambda b, c: (b, c, base + 2)),
            pl.BlockSpec((None, CHUNK, hv), lambda b, c: (b, c, base + 3)),
            pl.BlockSpec((CHUNK, RET_DK // 2), lambda b, c: (c, 0)),
            pl.BlockSpec((CHUNK, RET_DK // 2), lambda b, c: (c, 0)),
            pl.BlockSpec((RET_HEADS, 1, 1), lambda b, c: (0, 0, 0)),
            pl.BlockSpec((RET_HEADS, 1, RET_DV), lambda b, c: (0, 0, 0)),
        ],
        out_specs=pl.BlockSpec((None, CHUNK, hv), lambda b, c: (b, c, 0)),
        out_shape=jax.ShapeDtypeStruct((nb, lpad, hv), BF16),
        scratch_shapes=[pltpu.VMEM((RET_HEADS, RET_DV, RET_DK), F32)],
        compiler_params=_params(("parallel", "arbitrary")),
        name="ret_scan",
    )(proj, proj, proj, proj, cos, sin, lg, nrm)


def _diff_attn_kernel(seq_valid_end, lam_init, lpad, q1_ref, q2_ref, k1_ref, k2_ref, vt_ref, lam_ref, nrm_ref,
                      o_ref, acc1_ref, acc2_ref):
    qi = pl.program_id(2)
    t = ATTN_TILE
    tq = ATTN_Q_TILE
    qs = (q1_ref[...], q2_ref[...])
    k_refs = (k1_ref, k2_ref)
    acc_refs = (acc1_ref, acc2_ref)
    q_chunk = lax.shift_right_logical(qi * tq + lax.broadcasted_iota(jnp.int32, (1, tq), 1), 6)
    nt_dims = (((1,), (1,)), ((), ()))

    def start(j):
        return j * t if isinstance(j, int) else pl.multiple_of(j * t, t)

    def scores(j, masked):
        out = []
        if masked:
            k_idx = j * t + lax.broadcasted_iota(jnp.int32, (t, 1), 0)
            k_chunk = jnp.where(k_idx >= META_OFFSET,
                                jnp.where(k_idx < seq_valid_end, lax.shift_right_logical(k_idx, 6), 2**30), 2**30)
            mask = k_chunk <= q_chunk
        for b in range(2):
            s = lax.dot_general(k_refs[b][pl.ds(start(j), t), :], qs[b], nt_dims, preferred_element_type=F32)
            out.append(jnp.where(mask, s, -1e30) if masked else s)
        return tuple(out)

    def update(j, stats, s_cur):
        vt = vt_ref[:, pl.ds(start(j), t)]
        new = []
        for b in range(2):
            m, l = stats[b]
            s = s_cur[b]
            m_new = jnp.maximum(m, jnp.max(s, axis=0, keepdims=True))
            alpha = jnp.exp2(m - m_new)
            p = jnp.exp2(s - m_new)
            l_new = alpha * l + jnp.sum(p, axis=0, keepdims=True)
            acc_refs[b][...] = alpha * acc_refs[b][...] + jnp.dot(vt, p.astype(BF16), preferred_element_type=F32)
            new.append((m_new, l_new))
        return tuple(new)

    def masked_body(j, carry):
        stats, s_cur = carry
        s_next = scores(j + 1, True)
        return update(j, stats, s_cur), s_next

    def pair_body(p, carry):
        stats, s_cur = carry
        j = 2 * p
        s_mid = scores(j + 1, False)
        stats = update(j, stats, s_cur)
        s_next = scores(j + 2, False)
        return update(j + 1, stats, s_mid), s_next

    for b in range(2):
        acc_refs[b][...] = jnp.zeros_like(acc_refs[b])
    init = (jnp.full((1, tq), -1e30, F32), jnp.zeros((1, tq), F32))
    carry = ((init, init), scores(0, True))
    first_diag = qi * (tq // t)
    n_tiles = jnp.minimum(first_diag + tq // t, lpad // t)
    n_plain = jnp.maximum(first_diag - 1, 0)
    n_pairs = lax.shift_right_logical(n_plain, 1)
    carry = lax.fori_loop(0, n_pairs, pair_body, carry)
    carry = lax.fori_loop(2 * n_pairs, n_tiles - 1, masked_body, carry)
    (_, l1), (_, l2) = update(n_tiles - 1, *carry)

    lam_v = lam_ref[...]
    lam = (jnp.exp(jnp.sum(lam_v[0:1] * lam_v[1:2], axis=-1, keepdims=True))
           - jnp.exp(jnp.sum(lam_v[2:3] * lam_v[3:4], axis=-1, keepdims=True)) + lam_init)
    o_t = acc1_ref[...] * (1.0 / l1) - acc2_ref[...] * (lam * (1.0 / l2))
    o_ref[...] = (_rms(o_t.T, nrm_ref[...]) * (1.0 - lam_init)).astype(o_ref.dtype)


def _diff_attn(qk, vt, lam_vecs, nrm, lpad, seq_valid_end, lam_init):
    nb = qk.shape[1]
    t = ATTN_TILE
    tq = ATTN_Q_TILE
    kb = 2 * DIFF_HEADS
    return pl.pallas_call(
        functools.partial(_diff_attn_kernel, seq_valid_end, lam_init, lpad),
        grid=(nb, DIFF_HEADS, pl.cdiv(lpad, tq)),
        in_specs=[
            pl.BlockSpec((None, None, tq, DIFF_DK), lambda b, h, i: (2 * h, b, i, 0)),
            pl.BlockSpec((None, None, tq, DIFF_DK), lambda b, h, i: (2 * h + 1, b, i, 0)),
            pl.BlockSpec((None, None, lpad, DIFF_DK), lambda b, h, i: (kb + 2 * h, b, 0, 0)),
            pl.BlockSpec((None, None, lpad, DIFF_DK), lambda b, h, i: (kb + 2 * h + 1, b, 0, 0)),
            pl.BlockSpec((DIFF_DV, lpad), lambda b, h, i: (h, b)),
            pl.BlockSpec((4, DIFF_DK), lambda b, h, i: (0, 0)),
            pl.BlockSpec((1, DIFF_DV), lambda b, h, i: (0, 0)),
        ],
        out_specs=pl.BlockSpec((None, tq, DIFF_DV), lambda b, h, i: (b, i, h)),
        out_shape=jax.ShapeDtypeStruct((nb, lpad, DIFF_HEADS * DIFF_DV), BF16),
        scratch_shapes=[pltpu.VMEM((DIFF_DV, tq), F32), pltpu.VMEM((DIFF_DV, tq), F32)],
        compiler_params=_params(("parallel", "parallel", "arbitrary")),
        name="diff_attn",
    )(qk, qk, qk, qk, vt, lam_vecs, nrm)


def _ffn_kernel(x_ref, halo_ref, g_ref, wv_ref, wg_ref, cwv_ref, cwg_ref, cbv_ref, cbg_ref, wd_ref, keep_ref,
                o_ref, hn_ref, acc_ref, *u_refs):
    i = pl.program_id(0)
    j = pl.program_id(1)
    tm = x_ref.shape[0]

    @pl.when(j == 0)
    def _():
        g = g_ref[...]
        hn_ref[HALO:, :] = _rms(x_ref[...], g).astype(BF16)
        first = jnp.where(i > 0, 1.0, 0.0)
        hn_ref[:HALO, :] = (_rms(halo_ref[...], g) * first).astype(BF16)
        acc_ref[...] = jnp.zeros_like(acc_ref)

    hn = hn_ref[...]

    def conv(u_ref, cw_ref, cb_ref, sl):
        cw = cw_ref[:, sl]
        c = cb_ref[:, sl] + cw[0:1] * u_ref[pl.ds(HALO - 2, tm), :]
        c = c + cw[1:2] * u_ref[pl.ds(HALO - 1, tm), :]
        return c + cw[2:3] * u_ref[pl.ds(HALO, tm), :]

    subs = [slice(s * FF_SUB, (s + 1) * FF_SUB) for s in range(FF_TILE // FF_SUB)]
    bufs = [(u_refs[2 * s], u_refs[2 * s + 1]) for s in range(len(subs))]
    for (uv_ref, ug_ref), sl in zip(bufs, subs):
        uv_ref[...] = jnp.dot(hn, wv_ref[:, sl], preferred_element_type=F32)
        ug_ref[...] = jnp.dot(hn, wg_ref[:, sl], preferred_element_type=F32)
    total = None
    for (uv_ref, ug_ref), sl in zip(bufs, subs):
        gate = conv(ug_ref, cwg_ref, cbg_ref, sl)
        act = (conv(uv_ref, cwv_ref, cbv_ref, sl) * (gate * _sigmoid(gate))).astype(BF16)
        part = jnp.dot(act, wd_ref[sl, :], preferred_element_type=F32)
        total = part if total is None else total + part
    acc_ref[...] += total

    @pl.when(j == pl.num_programs(1) - 1)
    def _():
        o_ref[...] = x_ref[...] + acc_ref[...] * keep_ref[...]


def _ffn(x, g, w_up, conv_w, conv_b, w_down, keep):
    m, d = x.shape
    f = w_down.shape[0]
    nf = f // FF_TILE
    tm = ROW_TILE
    halo_blocks = tm // HALO
    return pl.pallas_call(
        _ffn_kernel,
        grid=(m // tm, nf),
        in_specs=[
            pl.BlockSpec((tm, d), lambda i, j: (i, 0)),
            pl.BlockSpec((None, HALO, d), lambda i, j: (jnp.maximum(i * halo_blocks - 1, 0), 0, 0)),
            pl.BlockSpec((1, d), lambda i, j: (0, 0)),
            pl.BlockSpec((d, FF_TILE), lambda i, j: (0, j)),
            pl.BlockSpec((d, FF_TILE), lambda i, j: (0, nf + j)),
            pl.BlockSpec((CONV_W, FF_TILE), lambda i, j: (0, j)),
            pl.BlockSpec((CONV_W, FF_TILE), lambda i, j: (0, nf + j)),
            pl.BlockSpec((1, FF_TILE), lambda i, j: (0, j)),
            pl.BlockSpec((1, FF_TILE), lambda i, j: (0, nf + j)),
            pl.BlockSpec((FF_TILE, d), lambda i, j: (j, 0)),
            pl.BlockSpec((tm, 1), lambda i, j: (i, 0)),
        ],
        out_specs=pl.BlockSpec((tm, d), lambda i, j: (i, 0)),
        out_shape=jax.ShapeDtypeStruct((m, d), F32),
        scratch_shapes=[pltpu.VMEM((tm + HALO, d), BF16), pltpu.VMEM((tm, d), F32)]
        + [pltpu.VMEM((tm + HALO, FF_SUB), F32)] * (2 * FF_TILE // FF_SUB),
        compiler_params=_params(("parallel", "arbitrary")),
        name="conv_ffn",
    )(x, x.reshape(m // HALO, HALO, d), g.reshape(1, d), w_up, w_up, conv_w, conv_w,
      conv_b.reshape(1, -1), conv_b.reshape(1, -1), w_down, keep)


def _rope_tables(pos, inv_freq):
    ang = pos.astype(F32)[:, None] * inv_freq[None, :]
    return jnp.cos(ang), jnp.sin(ang)


def kernel(x, meta, mix_norm_e, w_in_e, gla_w_gate_e, gla_b_gate_e, gla_norm_e, ret_norm_e, w_out_e, mix_norm_o, w_in_o, q_norm_o, k_norm_o, lam_q1_o, lam_k1_o, lam_q2_o, lam_k2_o, diff_norm_o, w_out_o, ffn_norm, w_up, conv_w, conv_b, w_down):
    nb, s, d = x.shape
    seq_valid_end = CHUNK + s
    lpad = -(-seq_valid_end // ATTN_TILE) * ATTN_TILE
    m = nb * lpad

    h = jnp.concatenate([jnp.zeros((nb, META_OFFSET, d), x.dtype),
                         jnp.broadcast_to(meta.astype(x.dtype)[None], (nb, N_META, d)),
                         x,
                         jnp.zeros((nb, lpad - seq_valid_end, d), x.dtype)], axis=1).reshape(m, d)
    idx = jnp.arange(lpad)
    valid = (idx >= META_OFFSET) & (idx < seq_valid_end)
    pos = idx - META_OFFSET
    keep = jnp.tile(valid.astype(F32), nb).reshape(m, 1)

    gla_cols = 2 * GLA_HEADS * GLA_DK + 2 * GLA_HEADS * GLA_DV
    w_in = w_in_e[0]
    w_main = jnp.concatenate([w_in[:, :gla_cols], w_in[:, gla_cols + GLA_GATE_RANK:]], axis=1).astype(BF16)
    w_lr = jnp.pad(w_in[:, gla_cols:gla_cols + GLA_GATE_RANK], ((0, 0), (0, LANES - GLA_GATE_RANK))).astype(BF16)
    w_gate2 = jnp.pad(gla_w_gate_e[0], ((0, LANES - GLA_GATE_RANK), (0, 0))).astype(BF16)
    proj, lr = _norm_matmul(h, mix_norm_e[0], w_main, w_lr, 3584)
    proj = proj.reshape(nb, lpad, -1)
    lr = lr.reshape(nb, lpad, LANES)
    sel = (jnp.arange(CHUNK)[:, None, None] == jnp.arange(CHUNK)[None, None, :]).astype(BF16)
    sel = jnp.broadcast_to(sel, (CHUNK, GLA_DK, CHUNK)).reshape(CHUNK // PAIR_BLOCK, PAIR_BLOCK * GLA_DK, CHUNK)
    o_a = _gla(proj, lr, w_gate2, gla_b_gate_e[0].reshape(1, -1), gla_norm_e[0].reshape(GLA_HEADS, 1, GLA_DV),
               sel, lpad, seq_valid_end)
    inv_freq_ret = 1.0 / (ROPE_THETA ** jnp.linspace(0.0, 1.0, RET_DK // 2, dtype=F32))
    cos_r, sin_r = _rope_tables(pos, inv_freq_ret)
    log_gamma = jnp.log(1.0 - 2.0 ** (-5.0 - jnp.arange(RET_HEADS, dtype=F32))).reshape(RET_HEADS, 1, 1)
    o_b = _ret(proj, cos_r, sin_r, log_gamma, ret_norm_e[0].reshape(RET_HEADS, 1, RET_DV), lpad, seq_valid_end)
    h = _matmul_res([o_a.reshape(m, -1), o_b.reshape(m, -1)], w_out_e[0].astype(BF16), h, keep)
    h = _ffn(h, ffn_norm[0], w_up[0].astype(BF16), conv_w[0], conv_b[0], w_down[0].astype(BF16), keep)

    layer = 1
    lam_init = 0.8 - 0.6 * math.exp(-0.3 * layer)
    qk_cols = 4 * DIFF_HEADS * DIFF_DK
    inv_freq = 1.0 / (ROPE_THETA ** (jnp.arange(0, DIFF_DK, 2, dtype=F32) / DIFF_DK))
    cos_d, sin_d = _rope_tables(pos, inv_freq)
    cos_full = jnp.tile(jnp.concatenate([cos_d, cos_d], axis=-1), (nb, 1))
    sin_signed = jnp.tile(jnp.concatenate([-sin_d, sin_d], axis=-1), (nb, 1))
    gains = jnp.stack([q_norm_o[0] * (DIFF_DK ** -0.5 * LOG2E), k_norm_o[0]]).reshape(2, 1, DIFF_DK)
    qk = _norm_matmul_qk(h, mix_norm_o[0], w_in_o[0][:, :qk_cols].astype(BF16), cos_full, sin_signed, gains,
                         2048).reshape(qk_cols // DIFF_DK, nb, lpad, DIFF_DK)
    vt = _norm_matmul_t(h, mix_norm_o[0], w_in_o[0][:, qk_cols:].T.astype(BF16), 2048)
    lam_vecs = jnp.stack([lam_q1_o[0], lam_k1_o[0], lam_q2_o[0], lam_k2_o[0]]).astype(F32)
    o = _diff_attn(qk, vt, lam_vecs, diff_norm_o[0].reshape(1, -1), lpad, seq_valid_end, lam_init)
    h = _matmul_res([o.reshape(m, -1)], w_out_o[0].astype(BF16), h, keep)
    h = _ffn(h, ffn_norm[1], w_up[1].astype(BF16), conv_w[1], conv_b[1], w_down[1].astype(BF16), keep)

    return h.reshape(nb, lpad, d)[:, CHUNK:CHUNK + s]
```
